```python
import math
import jax
import jax.numpy as jnp
from jax import lax
import numpy as np

D_MODEL = 1024
BATCH = 8
SEQ = 2048
DEPTH = 2
DEC_BATCH = 32
DEC_SEQ = 8
PAST_LEN = 16384
PAGE_SIZE = 128

D_FF = ((8 * D_MODEL // 3 + 127) // 128) * 128
SGU_CHUNK = 128
SGU_GROUPS = 4
D_A = D_MODEL // 2
SGU_CG = D_A // SGU_GROUPS
GDN_HEADS = 4
GDN_DK = 128
GDN_DV = 128
GDN_CHUNK = 64
CONV_W = 4
CONV_DIM = GDN_HEADS * (2 * GDN_DK + GDN_DV)
FOX_HEADS = 8
FOX_HD = 64
Q_BLOCK = 128
N_BRANCH = 3
ALPHA = (2.0 * DEPTH) ** 0.25
BETA_INIT = (8.0 * DEPTH) ** -0.25
LN_EPS = 1e-5
NORM_EPS = 1e-6
IN_SPLITS = (D_A, D_A, CONV_DIM, GDN_HEADS, GDN_HEADS, GDN_HEADS * GDN_DV,
             3 * FOX_HEADS * FOX_HD, FOX_HEADS, N_BRANCH * D_MODEL)
N_IN = sum(IN_SPLITS)
SPLIT_AT = tuple(sum(IN_SPLITS[:i + 1]) for i in range(len(IN_SPLITS) - 1))

kernel_name = 'hybrid_gmlp_gdn_fox_decoder_step'


def layer_norm(x, g, b):
    xf = x.astype(jnp.float32)
    mu = jnp.mean(xf, axis=-1, keepdims=True)
    var = jnp.mean(jnp.square(xf - mu), axis=-1, keepdims=True)
    return ((xf - mu) * lax.rsqrt(var + LN_EPS)).astype(x.dtype) * g + b


def swiglu(x, w1, w3, w2):
    return (jax.nn.silu(x @ w1) * (x @ w3)) @ w2


def l2norm(x):
    xf = x.astype(jnp.float32)
    return xf * lax.rsqrt(jnp.sum(xf * xf, axis=-1, keepdims=True) + NORM_EPS)


def causal_conv(x, buf, w):
    l = x.shape[1]
    xp = jnp.concatenate([buf.astype(x.dtype), x], axis=1)
    y = xp[:, 0:l] * w[0]
    for i in range(1, CONV_W):
        y = y + xp[:, i:i + l] * w[i]
    return jax.nn.silu(y), xp[:, -(CONV_W - 1):]


def spatial_gate(u, vn, w_s, b_s):
    b, l, _ = vn.shape
    rows = min(l, SGU_CHUNK)
    n = l // rows
    w = jnp.where(jnp.tril(jnp.ones((rows, rows), dtype=bool)), w_s[:, :rows, :rows], 0.0)
    vr = vn.reshape(b, n, rows, SGU_GROUPS, SGU_CG)
    mix = jnp.einsum('gts,bnsgc->bntgc', w, vr) + b_s[:, :rows].T[:, :, None]
    return u * mix.reshape(b, l, D_A)


def _to_chunks(t, n, c):
    b = t.shape[0]
    t = t.reshape((b, n, c) + t.shape[2:])
    return jnp.transpose(t, (1, 0, 3, 2) + tuple(range(4, t.ndim)))


def gated_delta_chunked(q, k, v, g, beta, s0, chunk):
    b, l, h, _ = q.shape
    dv = v.shape[-1]
    n = l // chunk
    q, k, v, g, beta = (_to_chunks(t, n, chunk) for t in (q, k, v, g, beta))
    gc = lax.cumsum(g, axis=3)
    incl = jnp.tril(jnp.ones((chunk, chunk), dtype=bool))
    strict = jnp.tril(jnp.ones((chunk, chunk), dtype=bool), k=-1)
    decay = jnp.exp(jnp.where(incl, gc[..., :, None] - gc[..., None, :], -jnp.inf))
    kk = jnp.einsum('nbhid,nbhjd->nbhij', k, k)
    a_mat = jnp.eye(chunk, dtype=q.dtype) + jnp.where(strict, beta[..., :, None] * kk * decay, 0.0)
    rhs = jnp.concatenate([v * beta[..., None], k * (beta * jnp.exp(gc))[..., None]], axis=-1)
    sol = lax.linalg.triangular_solve(a_mat, rhs, left_side=True, lower=True, unit_diagonal=True)
    w_val, k_cum = sol[..., :dv], sol[..., dv:]
    qk = jnp.einsum('nbhid,nbhjd->nbhij', q, k) * decay
    q_dec = q * jnp.exp(gc)[..., None]
    k_tail = k * jnp.exp(gc[..., -1:] - gc)[..., None]
    g_last = jnp.exp(gc[..., -1])

    def step(s, xs):
        w_c, kc_c, qk_c, qd_c, kt_c, gl_c = xs
        v_new = w_c - jnp.einsum('bhck,bhkv->bhcv', kc_c, s)
        o = jnp.einsum('bhck,bhkv->bhcv', qd_c, s) + jnp.einsum('bhij,bhjv->bhiv', qk_c, v_new)
        s = s * gl_c[..., None, None] + jnp.einsum('bhck,bhcv->bhkv', kt_c, v_new)
        return s, o

    s_fin, o = lax.scan(step, s0, (w_val, k_cum, qk, q_dec, k_tail, g_last))
    o = jnp.transpose(o, (1, 0, 3, 2, 4)).reshape(b, l, h, dv)
    return o, s_fin


def fox_prompt(q, k, v, logf):
    b, s_len, h, hd = q.shape
    scale = hd ** -0.5
    c = lax.cumsum(logf, axis=1)
    c_t = jnp.transpose(c, (0, 2, 1))
    nq = s_len // Q_BLOCK
    qb = q.reshape(b, nq, Q_BLOCK, h, hd).swapaxes(0, 1)
    cb = c.reshape(b, nq, Q_BLOCK, h).swapaxes(0, 1)
    kpos = jnp.arange(s_len)

    def one_block(args):
        qi, ci, i = args
        sc = jnp.einsum('bqhd,bkhd->bhqk', qi, k).astype(jnp.float32) * scale
        sc = sc + jnp.transpose(ci, (0, 2, 1))[..., None] - c_t[:, :, None, :]
        qpos = i * Q_BLOCK + jnp.arange(Q_BLOCK)
        sc = jnp.where(qpos[:, None] >= kpos[None, :], sc, -jnp.inf)
        p = jax.nn.softmax(sc, axis=-1).astype(v.dtype)
        return jnp.einsum('bhqk,bkhd->bqhd', p, v)

    o = lax.map(one_block, (qb, cb, jnp.arange(nq)))
    return o.swapaxes(0, 1).reshape(b, s_len, h, hd)


def fox_sample(q, k, v, logf, k_past, v_past, logf_past):
    hd = q.shape[-1]
    scale = hd ** -0.5
    ds = q.shape[1]
    cn = jnp.transpose(lax.cumsum(logf, axis=1), (0, 2, 1))
    lp = logf_past.astype(jnp.float32)
    r = jnp.transpose(lax.cumsum(lp, axis=1, reverse=True) - lp, (0, 2, 1))
    s_past = jnp.einsum('bqhd,bkhd->bhqk', q, k_past).astype(jnp.float32) * scale
    s_past = s_past + cn[..., :, None] + r[:, :, None, :]
    s_new = jnp.einsum('bqhd,bkhd->bhqk', q, k).astype(jnp.float32) * scale
    s_new = s_new + cn[..., :, None] - cn[..., None, :]
    s_new = jnp.where(jnp.tril(jnp.ones((ds, ds), dtype=bool)), s_new, -jnp.inf)
    p = jax.nn.softmax(jnp.concatenate([s_past, s_new], axis=-1), axis=-1).astype(v.dtype)
    n_past = k_past.shape[1]
    return (jnp.einsum('bhqk,bkhd->bqhd', p[..., :n_past], v_past.astype(v.dtype))
            + jnp.einsum('bhqk,bkhd->bqhd', p[..., n_past:], v))


def token_mixer(h, p, conv_buf, s0, fox_past):
    b, l, _ = h.shape
    z = h @ p['w_in']
    za_u, za_v, zb_qkv, zb_beta, zb_a, zb_z, zc_qkv, zc_f, z_gate = jnp.split(z, SPLIT_AT, axis=-1)
    u = jax.nn.gelu(za_u)
    vn = layer_norm(jax.nn.gelu(za_v), p['sgu_ln_g'], p['sgu_ln_b'])
    y_a = spatial_gate(u, vn, p['sgu_w'], p['sgu_b'])
    qkv, new_conv = causal_conv(zb_qkv, conv_buf, p['gdn_conv_w'])
    qb, kb, vb = jnp.split(qkv, (GDN_HEADS * GDN_DK, 2 * GDN_HEADS * GDN_DK), axis=-1)
    qb = l2norm(qb.reshape(b, l, GDN_HEADS, GDN_DK)) * GDN_DK ** -0.5
    kb = l2norm(kb.reshape(b, l, GDN_HEADS, GDN_DK))
    vb = vb.reshape(b, l, GDN_HEADS, GDN_DV).astype(jnp.float32)
    beta = jax.nn.sigmoid(zb_beta.astype(jnp.float32))
    g = -jnp.exp(p['gdn_a_log'].astype(jnp.float32)) * jax.nn.softplus(
        zb_a.astype(jnp.float32) + p['gdn_dt_bias'].astype(jnp.float32))
    chunk = GDN_CHUNK if l % GDN_CHUNK == 0 else l
    o_b, s_new = gated_delta_chunked(qb, kb, vb, g, beta, s0.astype(jnp.float32), chunk)
    gate_b = jax.nn.silu(zb_z.reshape(b, l, GDN_HEADS, GDN_DV).astype(jnp.float32))
    o_b = (o_b * lax.rsqrt(jnp.mean(o_b * o_b, axis=-1, keepdims=True) + NORM_EPS)
           * p['gdn_norm_g'].astype(jnp.float32) * gate_b)
    y_b = o_b.reshape(b, l, GDN_HEADS * GDN_DV).astype(h.dtype)
    qc, kc, vc = (t.reshape(b, l, FOX_HEADS, FOX_HD) for t in jnp.split(zc_qkv, 3, axis=-1))
    logf = jax.nn.log_sigmoid((zc_f + p['fox_f_bias']).astype(jnp.float32))
    if fox_past is None:
        o_c = fox_prompt(qc, kc, vc, logf)
    else:
        o_c = fox_sample(qc, kc, vc, logf, fox_past[0], fox_past[1], fox_past[2])
    y_c = o_c.reshape(b, l, FOX_HEADS * FOX_HD)
    g_a, g_b, g_c = jnp.split(jax.nn.sigmoid(z_gate + p['gate_bias']), N_BRANCH, axis=-1)
    merged = g_a * (y_a @ p['proj_a']) + g_b * (y_b @ p['proj_b']) + g_c * (y_c @ p['proj_c'])
    return merged @ p['w_out'], (kc, vc, logf, new_conv, s_new, vn)


def trunk_layer(x, p, conv_buf, s0, fox_past):
    x = layer_norm(ALPHA * x + 0.5 * swiglu(x, p['ffn1_w1'], p['ffn1_w3'], p['ffn1_w2']), p['ln1_g'], p['ln1_b'])
    mix, st = token_mixer(x, p, conv_buf, s0, fox_past)
    x = layer_norm(ALPHA * x + mix, p['ln2_g'], p['ln2_b'])
    x = layer_norm(ALPHA * x + 0.5 * swiglu(x, p['ffn2_w1'], p['ffn2_w3'], p['ffn2_w2']), p['ln3_g'], p['ln3_b'])
    return x, st


def setup_inputs(seed: int = 0) -> dict:
    key = jax.random.key(seed)
    keys = iter(jax.random.split(key, 64))

    def nrm(shape, scale):
        return jax.random.normal(next(keys), shape, jnp.float32) * scale

    def gain(shape):
        return 1.0 + nrm(shape, 0.02)

    n_pages = PAST_LEN // PAGE_SIZE
    n_pool = (5 * DEC_BATCH * n_pages) // 4
    perm = jax.random.permutation(next(keys), n_pool)
    page_table = perm[:DEC_BATCH * n_pages].reshape(DEC_BATCH, n_pages).astype(jnp.int32)
    dt = jnp.exp(jax.random.uniform(next(keys), (DEPTH, GDN_HEADS), jnp.float32,
                                    math.log(1e-3), math.log(1e-1)))
    dt_bias = dt + jnp.log(-jnp.expm1(-dt))
    a_log = jnp.log(jax.random.uniform(next(keys), (DEPTH, GDN_HEADS), jnp.float32, 1.0, 16.0))
    return {
        'x_prompt': nrm((BATCH, SEQ, D_MODEL), 1.0),
        'x_sample': nrm((DEC_BATCH, DEC_SEQ, D_MODEL), 1.0),
        'cache_k': nrm((DEPTH, n_pool, PAGE_SIZE, FOX_HEADS, FOX_HD), 1.0),
        'cache_v': nrm((DEPTH, n_pool, PAGE_SIZE, FOX_HEADS, FOX_HD), 1.0),
        'cache_logf': jax.nn.log_sigmoid(nrm((DEPTH, n_pool, PAGE_SIZE, FOX_HEADS), 1.0) + 3.0),
        'page_table': page_table,
        'state_conv': nrm((DEPTH, DEC_BATCH, CONV_W - 1, CONV_DIM), 1.0),
        'state_gdn': nrm((DEPTH, DEC_BATCH, GDN_HEADS, GDN_DK, GDN_DV), 0.1),
        'ln1_g': gain((DEPTH, D_MODEL)),
        'ln1_b': nrm((DEPTH, D_MODEL), 0.02),
        'ffn1_w1': nrm((DEPTH, D_MODEL, D_FF), D_MODEL ** -0.5),
        'ffn1_w3': nrm((DEPTH, D_MODEL, D_FF), D_MODEL ** -0.5),
        'ffn1_w2': nrm((DEPTH, D_FF, D_MODEL), BETA_INIT * D_FF ** -0.5),
        'w_in': nrm((DEPTH, D_MODEL, N_IN), D_MODEL ** -0.5),
        'sgu_ln_g': gain((DEPTH, D_A)),
        'sgu_ln_b': nrm((DEPTH, D_A), 0.02),
        'sgu_w': nrm((DEPTH, SGU_GROUPS, SGU_CHUNK, SGU_CHUNK), 0.5 * SGU_CHUNK ** -0.5),
        'sgu_b': 1.0 + nrm((DEPTH, SGU_GROUPS, SGU_CHUNK), 0.1),
        'gdn_conv_w': nrm((DEPTH, CONV_W, CONV_DIM), CONV_W ** -0.5),
        'gdn_a_log': a_log,
        'gdn_dt_bias': dt_bias,
        'gdn_norm_g': gain((DEPTH, GDN_DV)),
        'fox_f_bias': jnp.linspace(1.0, 5.0, FOX_HEADS, dtype=jnp.float32)[None, :] + nrm((DEPTH, FOX_HEADS), 0.1),
        'gate_bias': nrm((DEPTH, N_BRANCH * D_MODEL), 0.02),
        'proj_a': nrm((DEPTH, D_A, D_MODEL), BETA_INIT * D_A ** -0.5),
        'proj_b': nrm((DEPTH, GDN_HEADS * GDN_DV, D_MODEL), BETA_INIT * (GDN_HEADS * GDN_DV) ** -0.5),
        'proj_c': nrm((DEPTH, FOX_HEADS * FOX_HD, D_MODEL), BETA_INIT * (FOX_HEADS * FOX_HD) ** -0.5),
        'w_out': nrm((DEPTH, D_MODEL, D_MODEL), BETA_INIT * D_MODEL ** -0.5),
        'ln2_g': gain((DEPTH, D_MODEL)),
        'ln2_b': nrm((DEPTH, D_MODEL), 0.02),
        'ffn2_w1': nrm((DEPTH, D_MODEL, D_FF), D_MODEL ** -0.5),
        'ffn2_w3': nrm((DEPTH, D_MODEL, D_FF), D_MODEL ** -0.5),
        'ffn2_w2': nrm((DEPTH, D_FF, D_MODEL), BETA_INIT * D_FF ** -0.5),
        'ln3_g': gain((DEPTH, D_MODEL)),
        'ln3_b': nrm((DEPTH, D_MODEL), 0.02),
    }


def reference(x_prompt, x_sample, cache_k, cache_v, cache_logf, page_table, state_conv, state_gdn,
              ln1_g, ln1_b, ffn1_w1, ffn1_w3, ffn1_w2, w_in, sgu_ln_g, sgu_ln_b, sgu_w, sgu_b,
              gdn_conv_w, gdn_a_log, gdn_dt_bias, gdn_norm_g, fox_f_bias, gate_bias,
              proj_a, proj_b, proj_c, w_out, ln2_g, ln2_b, ffn2_w1, ffn2_w3, ffn2_w2, ln3_g, ln3_b):
    bp = x_prompt.shape[0]
    db = x_sample.shape[0]
    past = page_table.shape[1] * cache_k.shape[2]
    yp, ys = x_prompt, x_sample
    kp_l, vp_l, fp_l, cp_l, sp_l = [], [], [], [], []
    ks_l, vs_l, fs_l, cs_l, ss_l, as_l = [], [], [], [], [], []
    for l in range(DEPTH):
        p = dict(ln1_g=ln1_g[l], ln1_b=ln1_b[l], ffn1_w1=ffn1_w1[l], ffn1_w3=ffn1_w3[l], ffn1_w2=ffn1_w2[l],
                 w_in=w_in[l], sgu_ln_g=sgu_ln_g[l], sgu_ln_b=sgu_ln_b[l], sgu_w=sgu_w[l], sgu_b=sgu_b[l],
                 gdn_conv_w=gdn_conv_w[l], gdn_a_log=gdn_a_log[l], gdn_dt_bias=gdn_dt_bias[l],
                 gdn_norm_g=gdn_norm_g[l], fox_f_bias=fox_f_bias[l], gate_bias=gate_bias[l],
                 proj_a=proj_a[l], proj_b=proj_b[l], proj_c=proj_c[l], w_out=w_out[l],
                 ln2_g=ln2_g[l], ln2_b=ln2_b[l], ffn2_w1=ffn2_w1[l], ffn2_w3=ffn2_w3[l], ffn2_w2=ffn2_w2[l],
                 ln3_g=ln3_g[l], ln3_b=ln3_b[l])
        conv0 = jnp.zeros((bp, CONV_W - 1, CONV_DIM), x_prompt.dtype)
        s0 = jnp.zeros((bp, GDN_HEADS, GDN_DK, GDN_DV), jnp.float32)
        yp, (kp, vp, fp, cp, sp, _) = trunk_layer(yp, p, conv0, s0, None)
        kp_l.append(kp); vp_l.append(vp); fp_l.append(fp); cp_l.append(cp); sp_l.append(sp)
        k_past = cache_k[l][page_table].reshape(db, past, FOX_HEADS, FOX_HD)
        v_past = cache_v[l][page_table].reshape(db, past, FOX_HEADS, FOX_HD)
        f_past = cache_logf[l][page_table].reshape(db, past, FOX_HEADS)
        ys, (ks, vs, fs, cs, ss, vns) = trunk_layer(ys, p, state_conv[l], state_gdn[l], (k_past, v_past, f_past))
        ks_l.append(ks); vs_l.append(vs); fs_l.append(fs); cs_l.append(cs); ss_l.append(ss); as_l.append(vns)
    return (yp, ys,
            jnp.stack(kp_l), jnp.stack(vp_l), jnp.stack(fp_l), jnp.stack(cp_l), jnp.stack(sp_l),
            jnp.stack(ks_l), jnp.stack(vs_l), jnp.stack(fs_l), jnp.stack(cs_l), jnp.stack(ss_l), jnp.stack(as_l))
```

```python
import functools
import math

import jax
import jax.numpy as jnp
from jax import lax
from jax.experimental import pallas as pl
from jax.experimental.pallas import tpu as pltpu

F32 = jnp.float32
BF16 = jnp.bfloat16

D_MODEL = 1024
DEPTH = 2
PAGE_SIZE = 128
D_FF = ((8 * D_MODEL // 3 + 127) // 128) * 128
SGU_CHUNK = 128
SGU_GROUPS = 4
D_A = D_MODEL // 2
GDN_HEADS = 4
GDN_DK = 128
GDN_DV = 128
GDN_CHUNK = 64
CONV_W = 4
CONV_DIM = GDN_HEADS * (2 * GDN_DK + GDN_DV)
FOX_HEADS = 8
FOX_HD = 64
FOX_DIM = FOX_HEADS * FOX_HD
N_BRANCH = 3
ALPHA = (2.0 * DEPTH) ** 0.25
LN_EPS = 1e-5
NORM_EPS = 1e-6

LANES = 128
SUBLANES = 8
VMEM_LIMIT_BYTES = 56 * 1024 * 1024

SEG_A = 2 * D_A
SEG_BQKV = CONV_DIM
SEG_BZ = GDN_HEADS * GDN_DV
SEG_SMALL = LANES
N_IN_PAD = SEG_A + SEG_BQKV + SEG_BZ + 3 * FOX_DIM + N_BRANCH * D_MODEL + SEG_SMALL
SMALL_F = 0
SMALL_BETA = FOX_HEADS
SMALL_A = FOX_HEADS + GDN_HEADS

HIGHEST = lax.Precision.HIGHEST


def _cparams(*sem):
    return pltpu.CompilerParams(dimension_semantics=sem, vmem_limit_bytes=VMEM_LIMIT_BYTES)


def _const_spec(shape):
    nd = len(shape)
    return pl.BlockSpec(shape, lambda *_: (0,) * nd, pipeline_mode=pl.Buffered(1))


def _layer_norm(x, g, b):
    mu = jnp.mean(x, axis=-1, keepdims=True)
    xc = x - mu
    var = jnp.mean(xc * xc, axis=-1, keepdims=True)
    return xc * lax.rsqrt(var + LN_EPS) * g + b


def _softplus(x):
    return jnp.maximum(x, 0.0) + jnp.log1p(jnp.exp(-jnp.abs(x)))


def _log_sigmoid(x):
    return jnp.minimum(x, 0.0) - jnp.log1p(jnp.exp(-jnp.abs(x)))


def _dot(a, b):
    return jnp.dot(a, b, preferred_element_type=F32)


def _dot_nt(a, b):
    return lax.dot_general(a, b, (((1,), (1,)), ((), ())), preferred_element_type=F32)


def _iota(shape, dim):
    return lax.broadcasted_iota(jnp.int32, shape, dim)


FFN_SPLITS = 2


def _ffn_ln_kernel(x_ref, w1_ref, w3_ref, w2_ref, g_ref, b_ref, o_ref):
    x = x_ref[...]
    xb = x.astype(BF16)
    fc = D_FF // FFN_SPLITS
    y = None
    for c in range(FFN_SPLITS):
        h1 = _dot(xb, w1_ref[:, c * fc:(c + 1) * fc])
        h3 = _dot(xb, w3_ref[:, c * fc:(c + 1) * fc])
        a = (h1 * jax.nn.sigmoid(h1) * h3).astype(BF16)
        part = _dot(a, w2_ref[c * fc:(c + 1) * fc, :])
        y = part if y is None else y + part
    o_ref[...] = _layer_norm(ALPHA * x + 0.5 * y, g_ref[...], b_ref[...])


def _ffn_ln(x, w1, w3, w2, g, b, tm):
    t = x.shape[0]
    row = pl.BlockSpec((tm, D_MODEL), lambda i: (i, 0))
    return pl.pallas_call(
        _ffn_ln_kernel,
        grid=(t // tm,),
        in_specs=[row, _const_spec(w1.shape), _const_spec(w3.shape), _const_spec(w2.shape),
                  _const_spec(g.shape), _const_spec(b.shape)],
        out_specs=row,
        out_shape=jax.ShapeDtypeStruct((t, D_MODEL), F32),
        compiler_params=_cparams("parallel"),
        name="ffn_ln",
    )(x, w1, w3, w2, g, b)


_IN_SEGS = (SEG_A, SEG_BQKV, SEG_BZ, FOX_DIM, FOX_DIM, FOX_DIM, N_BRANCH * D_MODEL, SEG_SMALL)
_IN_OFFS = tuple(sum(_IN_SEGS[:i]) for i in range(len(_IN_SEGS)))


def _in_proj_kernel(x_ref, w_ref, *refs, kv_transposed):
    xb = x_ref[...].astype(BF16)

    def seg(i):
        return _dot(xb, w_ref[:, _IN_OFFS[i]:_IN_OFFS[i] + _IN_SEGS[i]])

    if kv_transposed:
        wkvt_ref, za_ref, zq_ref, zz_ref, qb_ref, kt_ref, vt_ref, ktb_ref, vtb_ref, zg_ref, zs_ref = refs
        kt = _dot_nt(wkvt_ref[0:FOX_DIM, :], xb)
        kt_ref[...] = kt
        ktb_ref[...] = kt.astype(BF16)
        vt = _dot_nt(wkvt_ref[FOX_DIM:2 * FOX_DIM, :], xb)
        vt_ref[...] = vt
        vtb_ref[...] = vt.astype(BF16)
    else:
        za_ref, zq_ref, zz_ref, qb_ref, k_ref, v_ref, zg_ref, zs_ref = refs
        k_ref[...] = seg(4)
        v_ref[...] = seg(5)
    za_ref[...] = seg(0)
    zq_ref[...] = seg(1)
    zz_ref[...] = seg(2)
    qb_ref[...] = (seg(3) * (FOX_HD ** -0.5)).astype(BF16)
    zg_ref[...] = seg(6)
    zs_ref[...] = seg(7)


def _in_proj(x, w, wkvt, n_seq, seq_rows, tm):
    t = x.shape[0]
    row = lambda n: pl.BlockSpec((tm, n), lambda i: (i, 0))
    sds = jax.ShapeDtypeStruct
    head = [(row(SEG_A), sds((t, SEG_A), F32)), (row(SEG_BQKV), sds((t, SEG_BQKV), F32)),
            (row(SEG_BZ), sds((t, SEG_BZ), F32)), (row(FOX_DIM), sds((t, FOX_DIM), BF16))]
    tail = [(row(N_BRANCH * D_MODEL), sds((t, N_BRANCH * D_MODEL), F32)), (row(SEG_SMALL), sds((t, SEG_SMALL), F32))]
    in_specs = [row(D_MODEL), _const_spec(w.shape)]
    operands = [x, w]
    if wkvt is None:
        kv = [(row(FOX_DIM), sds((t, FOX_DIM), F32))] * 2
    else:
        nt = seq_rows // tm
        f32_spec = pl.BlockSpec((None, FOX_DIM, tm), lambda i: (i // nt, 0, i % nt))
        bf_spec = pl.BlockSpec((None, None, FOX_DIM, tm), lambda i: (i // nt, i % nt, 0, 0))
        kv = [(f32_spec, sds((n_seq, FOX_DIM, seq_rows), F32))] * 2 + [(bf_spec, sds((n_seq, nt, FOX_DIM, tm), BF16))] * 2
        in_specs.append(_const_spec(wkvt.shape))
        operands.append(wkvt)
    outs = head + kv + tail
    return pl.pallas_call(
        functools.partial(_in_proj_kernel, kv_transposed=wkvt is not None),
        grid=(t // tm,),
        in_specs=in_specs,
        out_specs=[o[0] for o in outs],
        out_shape=[o[1] for o in outs],
        compiler_params=_cparams("parallel"),
        name="in_proj",
    )(*operands)


def _sgu_kernel(za_ref, g_ref, b_ref, w_ref, bs_ref, ya_ref, *maybe_vn_ref, seq_rows, tm):
    u = jax.nn.gelu(za_ref[:, :D_A], approximate=True)
    vn = _layer_norm(jax.nn.gelu(za_ref[:, D_A:], approximate=True), g_ref[...], b_ref[...])
    if maybe_vn_ref:
        maybe_vn_ref[0][...] = vn
    r = _iota((SGU_CHUNK, SGU_CHUNK), 0)
    c = _iota((SGU_CHUNK, SGU_CHUNK), 1)
    keep = (r >= c) & ((r // seq_rows) == (c // seq_rows))
    cg = D_A // SGU_GROUPS
    for g in range(SGU_GROUPS):
        wg = jnp.where(keep, w_ref[g], 0.0).astype(BF16)
        bg = bs_ref[g]
        for n in range(tm // SGU_CHUNK):
            rows = slice(n * SGU_CHUNK, (n + 1) * SGU_CHUNK)
            cols = slice(g * cg, (g + 1) * cg)
            mix = _dot(wg, vn[rows, cols].astype(BF16)) + bg
            ya_ref[rows, cols] = u[rows, cols] * mix


def _sgu(za, ln_g, ln_b, w, bs, seq_rows, tm, emit_vn):
    t = za.shape[0]
    out_spec = pl.BlockSpec((tm, D_A), lambda i: (i, 0))
    out_shape = jax.ShapeDtypeStruct((t, D_A), F32)
    return pl.pallas_call(
        functools.partial(_sgu_kernel, seq_rows=seq_rows, tm=tm),
        grid=(t // tm,),
        in_specs=[pl.BlockSpec((tm, SEG_A), lambda i: (i, 0)), _const_spec(ln_g.shape), _const_spec(ln_b.shape),
                  _const_spec(w.shape), _const_spec(bs.shape)],
        out_specs=[out_spec, out_spec] if emit_vn else [out_spec],
        out_shape=[out_shape, out_shape] if emit_vn else [out_shape],
        compiler_params=_cparams("parallel"),
        name="sgu",
    )(za, ln_g, ln_b, w, bs)


PREP_ROWS = 256


def _fox_prep_kernel(zs_ref, fb_ref, lf_ref, ccol_ref, crow_ref, carry_ref, *, seq_rows):
    lf = _log_sigmoid(zs_ref[...] + fb_ref[...])
    if seq_rows > PREP_ROWS:
        lf_ref[...] = lf.T[0:FOX_HEADS, :]
    else:
        lf_ref[...] = lf
    r = _iota((PREP_ROWS, PREP_ROWS), 0)
    c = _iota((PREP_ROWS, PREP_ROWS), 1)
    span = min(seq_rows, PREP_ROWS)
    tri = jnp.where((r >= c) & ((r // span) == (c // span)), 1.0, 0.0).astype(F32)
    csum = jnp.dot(tri, lf, precision=HIGHEST, preferred_element_type=F32)
    if seq_rows > PREP_ROWS:
        blocks_per_seq = seq_rows // PREP_ROWS

        @pl.when(pl.program_id(0) % blocks_per_seq == 0)
        def _():
            carry_ref[...] = jnp.zeros_like(carry_ref)

        csum = csum + carry_ref[0:1, :]
        carry_ref[...] = jnp.broadcast_to(csum[PREP_ROWS - 1:PREP_ROWS, :], carry_ref.shape)
    ccol_ref[...] = csum
    crow_ref[0] = csum.T[0:FOX_HEADS, :]


def _fox_prep(zs, f_bias_row, seq_rows):
    t = zs.shape[0]
    nb = t // PREP_ROWS
    row = pl.BlockSpec((PREP_ROWS, LANES), lambda i: (i, 0))
    if seq_rows > PREP_ROWS:
        bps = seq_rows // PREP_ROWS
        lf_spec = pl.BlockSpec((None, FOX_HEADS, PREP_ROWS), lambda i: (i // bps, 0, i % bps))
        lf_shape = jax.ShapeDtypeStruct((t // seq_rows, FOX_HEADS, seq_rows), F32)
    else:
        lf_spec, lf_shape = row, jax.ShapeDtypeStruct((t, LANES), F32)
    return pl.pallas_call(
        functools.partial(_fox_prep_kernel, seq_rows=seq_rows),
        grid=(nb,),
        in_specs=[row, _const_spec(f_bias_row.shape)],
        out_specs=[lf_spec, row, pl.BlockSpec((1, FOX_HEADS, PREP_ROWS), lambda i: (i, 0, 0))],
        out_shape=[lf_shape, jax.ShapeDtypeStruct((t, LANES), F32),
                   jax.ShapeDtypeStruct((nb, FOX_HEADS, PREP_ROWS), F32)],
        scratch_shapes=[pltpu.VMEM((SUBLANES, LANES), F32)],
        compiler_params=_cparams("arbitrary"),
        name="fox_prep",
    )(zs, f_bias_row)


GDN_PAR_ALOG = 0
GDN_PAR_DTB = 1
GDN_PAR_NORM = 2


def _unit_lower_inverse_minus_identity(a):
    n = GDN_CHUNK
    r = _iota((n, n), 0)
    c = _iota((n, n), 1)

    def mm(x, y):
        return _dot(x.astype(BF16), y.astype(BF16))

    same16 = (r // 16) == (c // 16)
    same32 = (r // 32) == (c // 32)
    ad = jnp.where(same16, a, 0.0)
    t = -ad
    p = mm(ad, ad)
    t = t + p + mm(t, p)
    p = mm(p, p)
    t = t + p + mm(t, p)
    p = mm(p, p)
    t = t + p + mm(t, p)
    for off in (jnp.where(same32 & ~same16, a, 0.0), jnp.where(~same32, a, 0.0)):
        m = off + mm(t, off)
        t = t - m - mm(m, t)
    return t


def _gdn_kernel(zq_ref, zs_ref, zz_ref, conv0_ref, s0_ref, cw_ref, par_ref, y_ref, sfin_ref, xp_scr, s_scr,
                *, rows_in, rows, blocks_per_seq):
    blk = pl.program_id(1)
    nck = rows // GDN_CHUNK

    @pl.when(blk == 0)
    def _():
        xp_scr[0:SUBLANES, :] = conv0_ref[0]
        s_scr[...] = s0_ref[0]

    @pl.when(blk > 0)
    def _():
        xp_scr[0:SUBLANES, :] = xp_scr[rows:rows + SUBLANES, :]

    xp_scr[SUBLANES:SUBLANES + rows_in, :] = zq_ref[...]
    if rows_in < rows:
        xp_scr[SUBLANES + rows_in:SUBLANES + rows, :] = jnp.zeros((rows - rows_in, CONV_DIM), F32)

    base = SUBLANES - (CONV_W - 1)
    yc = xp_scr[base:base + rows, :] * cw_ref[0:1, :]
    for i in range(1, CONV_W):
        yc = yc + xp_scr[base + i:base + i + rows, :] * cw_ref[i:i + 1, :]
    yc = yc * jax.nn.sigmoid(yc)

    zs = zs_ref[...]
    if rows_in < rows:
        zs = jnp.concatenate([zs, jnp.zeros((rows - rows_in, LANES), F32)], axis=0)
    valid = _iota((rows, LANES), 0) < rows_in
    sig = jnp.where(valid, jax.nn.sigmoid(zs), 0.0)
    gl = -jnp.exp(par_ref[GDN_PAR_ALOG:GDN_PAR_ALOG + 1, :]) * _softplus(zs + par_ref[GDN_PAR_DTB:GDN_PAR_DTB + 1, :])
    gl = jnp.where(valid, gl, 0.0)

    n = GDN_CHUNK
    r = _iota((n, n), 0)
    c = _iota((n, n), 1)
    incl = r >= c
    strict = r > c
    tri = jnp.where(incl, 1.0, 0.0).astype(F32)
    norm_g = par_ref[GDN_PAR_NORM:GDN_PAR_NORM + 1, :]

    for ck in range(nck):
        rs = slice(ck * n, (ck + 1) * n)
        gc = jnp.dot(tri, gl[rs], precision=HIGHEST, preferred_element_type=F32)
        gct = gc.T
        for h in range(GDN_HEADS):
            q = yc[rs, h * GDN_DK:(h + 1) * GDN_DK]
            k = yc[rs, GDN_HEADS * GDN_DK + h * GDN_DK:GDN_HEADS * GDN_DK + (h + 1) * GDN_DK]
            v = yc[rs, 2 * GDN_HEADS * GDN_DK + h * GDN_DV:2 * GDN_HEADS * GDN_DK + (h + 1) * GDN_DV]
            q = q * lax.rsqrt(jnp.sum(q * q, axis=-1, keepdims=True) + NORM_EPS) * (GDN_DK ** -0.5)
            k = k * lax.rsqrt(jnp.sum(k * k, axis=-1, keepdims=True) + NORM_EPS)
            beta = sig[rs, SMALL_BETA + h:SMALL_BETA + h + 1]
            gcol = gc[:, SMALL_A + h:SMALL_A + h + 1]
            grow = gct[SMALL_A + h:SMALL_A + h + 1, :]
            glast = gc[n - 1:n, SMALL_A + h:SMALL_A + h + 1]
            decay = jnp.exp(jnp.where(incl, gcol - grow, -jnp.inf))
            egc = jnp.exp(gcol)
            kb = k.astype(BF16)
            qk_kk = _dot_nt(jnp.concatenate([q, k], axis=0).astype(BF16), kb)
            qk = qk_kk[:n] * decay
            a = jnp.where(strict, beta * qk_kk[n:] * decay, 0.0)
            tinv = _unit_lower_inverse_minus_identity(a)
            rhs = jnp.concatenate([v * beta, k * (beta * egc)], axis=-1)
            sol = rhs + _dot(tinv.astype(BF16), rhs.astype(BF16))
            w_val = sol[:, :GDN_DV]
            k_cum = sol[:, GDN_DV:]
            s = s_scr[h]
            sb = s.astype(BF16)
            v_new = w_val - _dot(k_cum.astype(BF16), sb)
            o = _dot((q * egc).astype(BF16), sb) + _dot(qk.astype(BF16), v_new.astype(BF16))
            k_tail = k * jnp.exp(glast - gcol)
            s_scr[h] = s * jnp.exp(glast) + _dot(k_tail.T.astype(BF16), v_new.astype(BF16))
            nv = min(rows_in, rs.stop) - rs.start
            zz = zz_ref[rs.start:rs.start + nv, h * GDN_DV:(h + 1) * GDN_DV]
            ov = o[:nv]
            gate = zz * jax.nn.sigmoid(zz)
            ob = ov * lax.rsqrt(jnp.mean(ov * ov, axis=-1, keepdims=True) + NORM_EPS) * norm_g * gate
            y_ref[rs.start:rs.start + nv, h * GDN_DV:(h + 1) * GDN_DV] = ob

    @pl.when(blk == blocks_per_seq - 1)
    def _():
        sfin_ref[0] = s_scr[...]


def _gdn(zq, zs, zz, conv0, s0, conv_w, par, n_seq, seq_rows, rows_in, rows):
    t = zq.shape[0]
    bps = seq_rows // rows_in

    def row(nc):
        return pl.BlockSpec((rows_in, nc), lambda b, j: (b * bps + j, 0))

    state = pl.BlockSpec((1, GDN_HEADS, GDN_DK, GDN_DV), lambda b, j: (b, 0, 0, 0))
    return pl.pallas_call(
        functools.partial(_gdn_kernel, rows_in=rows_in, rows=rows, blocks_per_seq=bps),
        grid=(n_seq, bps),
        in_specs=[row(CONV_DIM), row(LANES), row(SEG_BZ),
                  pl.BlockSpec((1, SUBLANES, CONV_DIM), lambda b, j: (b, 0, 0)), state,
                  _const_spec(conv_w.shape), _const_spec(par.shape)],
        out_specs=[row(SEG_BZ), state],
        out_shape=[jax.ShapeDtypeStruct((t, SEG_BZ), F32),
                   jax.ShapeDtypeStruct((n_seq, GDN_HEADS, GDN_DK, GDN_DV), F32)],
        scratch_shapes=[pltpu.VMEM((rows + 2 * SUBLANES, CONV_DIM), F32),
                        pltpu.VMEM((GDN_HEADS, GDN_DK, GDN_DV), F32)],
        compiler_params=_cparams("parallel", "arbitrary"),
        name="gdn",
    )(zq, zs, zz, conv0, s0, conv_w, par)


FOX_TQ = 256
FOX_TK = PREP_ROWS


def _fox_prompt_kernel(qb_ref, kt_ref, vt_ref, ccol_ref, crow_ref, o_ref):
    hp = pl.program_id(1)
    qi = pl.program_id(2)
    q2 = qb_ref[...]
    lane = _iota((FOX_TQ, LANES), 1)
    ccols = ccol_ref[...]
    rpos = _iota((FOX_TQ, FOX_TK), 0)
    cpos = _iota((FOX_TQ, FOX_TK), 1)
    qh = [q2[:, e * FOX_HD:(e + 1) * FOX_HD] for e in range(2)]
    ccol = [jnp.sum(jnp.where(lane == 2 * hp + e, ccols, 0.0), axis=-1, keepdims=True) for e in range(2)]

    def tile(j, carry, diagonal):
        out = []
        for e in range(2):
            m, l, acc = carry[e]
            kt = kt_ref[j, e * FOX_HD:(e + 1) * FOX_HD, :]
            vt = vt_ref[j, e * FOX_HD:(e + 1) * FOX_HD, :]
            s = _dot(qh[e], kt) + (ccol[e] - crow_ref[j, pl.ds(2 * hp + e, 1), :])
            if diagonal:
                s = jnp.where(rpos >= cpos, s, -jnp.inf)
            m_new = jnp.maximum(m, jnp.max(s, axis=-1, keepdims=True))
            alpha = jnp.exp(m - m_new)
            p = jnp.exp(s - m_new)
            l = alpha * l + jnp.sum(p, axis=-1, keepdims=True)
            acc = alpha * acc + _dot_nt(p.astype(BF16), vt)
            out.append((m_new, l, acc))
        return tuple(out)

    init = tuple((jnp.full((FOX_TQ, 1), -jnp.inf, F32), jnp.zeros((FOX_TQ, 1), F32), jnp.zeros((FOX_TQ, FOX_HD), F32))
                 for _ in range(2))
    carry = lax.fori_loop(0, qi, lambda j, cr: tile(j, cr, False), init)
    carry = tile(qi, carry, True)
    o_ref[...] = jnp.concatenate([carry[e][2] / carry[e][1] for e in range(2)], axis=-1)


def _fox_prompt(qb, ktb, vtb, ccol, crow, n_seq, seq_rows):
    t = qb.shape[0]
    nq = seq_rows // FOX_TQ
    nkv = seq_rows // FOX_TK
    qspec = pl.BlockSpec((FOX_TQ, LANES), lambda b, h, i: (b * nq + i, h))
    kvspec = pl.BlockSpec((None, nkv, LANES, FOX_TK), lambda b, h, i: (b, 0, h, 0))
    return pl.pallas_call(
        _fox_prompt_kernel,
        grid=(n_seq, FOX_HEADS // 2, nq),
        in_specs=[qspec, kvspec, kvspec,
                  pl.BlockSpec((FOX_TQ, LANES), lambda b, h, i: (b * nq + i, 0)),
                  pl.BlockSpec((None, nkv, FOX_HEADS, FOX_TK), lambda b, h, i: (b, 0, 0, 0))],
        out_specs=qspec,
        out_shape=jax.ShapeDtypeStruct((t, FOX_DIM), F32),
        compiler_params=_cparams("parallel", "parallel", "arbitrary"),
        name="fox_prompt",
    )(qb, ktb, vtb, ccol, crow)


FS_SLOTS = 3
FS_ROWS = FOX_HEADS * SUBLANES


def _fox_sample_kernel(pt_ref, qb_ref, kn_ref, vn_ref, ccol_ref, crow_ref, lfc_ref, ck_ref, cv_ref, o_ref,
                       kbuf, vbuf, lbuf, sem, *, n_pages, layer):
    b = pl.program_id(0)

    def copies(it, slot):
        page = pt_ref[b, n_pages - 1 - it]
        return (pltpu.make_async_copy(ck_ref.at[layer, page], kbuf.at[slot], sem.at[0, slot]),
                pltpu.make_async_copy(cv_ref.at[layer, page], vbuf.at[slot], sem.at[1, slot]),
                pltpu.make_async_copy(lfc_ref.at[layer, page], lbuf.at[slot], sem.at[2, slot]))

    for it in range(FS_SLOTS - 1):
        for cp in copies(it, it):
            cp.start()

    q8 = qb_ref[...]
    qrep = jnp.concatenate([q8] * FOX_HEADS, axis=0)
    rr = _iota((FS_ROWS, FOX_DIM), 0)
    cc = _iota((FS_ROWS, FOX_DIM), 1)
    own = (rr // SUBLANES) == (cc // FOX_HD)
    qbd = jnp.where(own, qrep, jnp.zeros_like(qrep))
    cn_col = jnp.concatenate([ccol_ref[:, h:h + 1] for h in range(FOX_HEADS)], axis=0)

    pr = _iota((PAGE_SIZE, PAGE_SIZE), 0)
    pc = _iota((PAGE_SIZE, PAGE_SIZE), 1)
    later = jnp.where(pr > pc, 1.0, 0.0).astype(F32)

    def rows8(x):
        return jnp.concatenate([jnp.broadcast_to(x[h:h + 1, :], (SUBLANES, x.shape[1])) for h in range(FOX_HEADS)], axis=0)

    def update(carry, s, vbf, v_transposed):
        m, l, acc = carry
        m_new = jnp.maximum(m, jnp.max(s, axis=-1, keepdims=True))
        alpha = jnp.exp(m - m_new)
        p = jnp.exp(s - m_new)
        l = alpha * l + jnp.sum(p, axis=-1, keepdims=True)
        pv = _dot_nt(p.astype(BF16), vbf) if v_transposed else _dot(p.astype(BF16), vbf)
        return m_new, l, alpha * acc + pv

    def body(it, carry):
        m, l, acc, tail = carry
        slot = it % FS_SLOTS
        nxt = it + FS_SLOTS - 1

        @pl.when(nxt < n_pages)
        def _():
            for cp in copies(nxt, nxt % FS_SLOTS):
                cp.start()

        for cp in copies(it, slot):
            cp.wait()
        lf = lbuf[slot]
        suffix = jnp.dot(lf, later, precision=HIGHEST, preferred_element_type=F32) + tail
        tail = tail + jnp.sum(lf, axis=-1, keepdims=True)
        s = _dot(qbd, kbuf[slot].astype(BF16)) + (rows8(suffix) + cn_col)
        m, l, acc = update((m, l, acc), s, vbuf[slot].astype(BF16), True)
        return m, l, acc, tail

    init = (jnp.full((FS_ROWS, 1), -jnp.inf, F32), jnp.zeros((FS_ROWS, 1), F32),
            jnp.zeros((FS_ROWS, FOX_DIM), F32), jnp.zeros((FOX_HEADS, 1), F32))
    m, l, acc, _ = lax.fori_loop(0, n_pages, body, init)

    pad = jnp.zeros((PAGE_SIZE - SUBLANES, FOX_DIM), F32)
    kn = jnp.concatenate([kn_ref[...], pad], axis=0).astype(BF16)
    vn = jnp.concatenate([vn_ref[...], pad], axis=0).astype(BF16)
    tok = _iota((FS_ROWS, PAGE_SIZE), 0) % SUBLANES
    pos = _iota((FS_ROWS, PAGE_SIZE), 1)
    s = _dot_nt(qbd, kn) + (cn_col - rows8(crow_ref[0]))
    s = jnp.where(pos <= tok, s, -jnp.inf)
    m, l, acc = update((m, l, acc), s, vn, False)

    o = acc / l
    out = jnp.zeros((SUBLANES, FOX_DIM), F32)
    hl = _iota((SUBLANES, FOX_DIM), 1) // FOX_HD
    for h in range(FOX_HEADS):
        out = jnp.where(hl == h, o[h * SUBLANES:(h + 1) * SUBLANES, :], out)
    o_ref[...] = out


def _fox_sample(page_table, qb, kn, vn, ccol, crow, lfc, ck, cv, layer):
    n_seq, n_pages = page_table.shape
    t = qb.shape[0]
    tok = lambda n, dt=None: pl.BlockSpec((SUBLANES, n), lambda b, pt: (b, 0))
    grid_spec = pltpu.PrefetchScalarGridSpec(
        num_scalar_prefetch=1,
        grid=(n_seq,),
        in_specs=[tok(FOX_DIM), tok(FOX_DIM), tok(FOX_DIM), tok(LANES),
                  pl.BlockSpec((1, FOX_HEADS, LANES), lambda b, pt: (b, 0, 0)),
                  pl.BlockSpec(memory_space=pl.ANY), pl.BlockSpec(memory_space=pl.ANY),
                  pl.BlockSpec(memory_space=pl.ANY)],
        out_specs=tok(FOX_DIM),
        scratch_shapes=[pltpu.VMEM((FS_SLOTS, FOX_DIM, PAGE_SIZE), F32),
                        pltpu.VMEM((FS_SLOTS, FOX_DIM, PAGE_SIZE), F32),
                        pltpu.VMEM((FS_SLOTS, FOX_HEADS, PAGE_SIZE), F32),
                        pltpu.SemaphoreType.DMA((3, FS_SLOTS))],
    )
    return pl.pallas_call(
        functools.partial(_fox_sample_kernel, n_pages=n_pages, layer=layer),
        grid_spec=grid_spec,
        out_shape=jax.ShapeDtypeStruct((t, FOX_DIM), F32),
        compiler_params=_cparams("arbitrary"),
        name="fox_sample",
    )(page_table, qb, kn, vn, ccol, crow, lfc, ck, cv)


def _merge_kernel(ya_ref, yb_ref, yc_ref, zg_ref, x_ref, gb_ref, pa_ref, pb_ref, pc_ref, wo_ref, g_ref, b_ref, o_ref):
    merged = None
    for i, (y_ref, p_ref) in enumerate(((ya_ref, pa_ref), (yb_ref, pb_ref), (yc_ref, pc_ref))):
        cols = slice(i * D_MODEL, (i + 1) * D_MODEL)
        gate = jax.nn.sigmoid(zg_ref[:, cols] + gb_ref[:, cols])
        term = gate * _dot(y_ref[...].astype(BF16), p_ref[...])
        merged = term if merged is None else merged + term
    mix = _dot(merged.astype(BF16), wo_ref[...])
    o_ref[...] = _layer_norm(ALPHA * x_ref[...] + mix, g_ref[...], b_ref[...])


def _merge(ya, yb, yc, zg, x, gate_bias, pa, pb, pc, wo, g, b, tm):
    t = x.shape[0]
    row = lambda n: pl.BlockSpec((tm, n), lambda i: (i, 0))
    consts = (gate_bias, pa, pb, pc, wo, g, b)
    return pl.pallas_call(
        _merge_kernel,
        grid=(t // tm,),
        in_specs=[row(D_A), row(SEG_BZ), row(FOX_DIM), row(N_BRANCH * D_MODEL), row(D_MODEL)]
        + [_const_spec(c.shape) for c in consts],
        out_specs=row(D_MODEL),
        out_shape=jax.ShapeDtypeStruct((t, D_MODEL), F32),
        compiler_params=_cparams("parallel"),
        name="merge",
    )(ya, yb, yc, zg, x, *consts)


_W_IN_K = 2 * D_A + CONV_DIM + 2 * GDN_HEADS + SEG_BZ + FOX_DIM


def _regroup_w_in(w_in):
    o_bqkv = 2 * D_A
    o_beta = o_bqkv + CONV_DIM
    o_a = o_beta + GDN_HEADS
    o_z = o_a + GDN_HEADS
    o_cqkv = o_z + SEG_BZ
    o_f = o_cqkv + 3 * FOX_DIM
    o_gate = o_f + FOX_HEADS
    small = jnp.concatenate([w_in[:, o_f:o_gate], w_in[:, o_beta:o_a], w_in[:, o_a:o_z],
                             jnp.zeros((D_MODEL, SEG_SMALL - FOX_HEADS - 2 * GDN_HEADS), w_in.dtype)], axis=1)
    return jnp.concatenate([w_in[:, :o_beta], w_in[:, o_z:o_cqkv], w_in[:, o_cqkv:o_f], w_in[:, o_gate:], small],
                           axis=1).astype(BF16)


def _lane_row(vals, offset):
    return jnp.zeros((1, LANES), F32).at[0, offset:offset + vals.shape[0]].set(vals.astype(F32))


def _layer_params(l, p):
    row = lambda a: a[l].reshape(1, -1)
    bf = lambda a: a[l].astype(BF16)
    par = jnp.zeros((SUBLANES, LANES), F32)
    par = par.at[GDN_PAR_ALOG, SMALL_A:SMALL_A + GDN_HEADS].set(p["gdn_a_log"][l])
    par = par.at[GDN_PAR_DTB, SMALL_A:SMALL_A + GDN_HEADS].set(p["gdn_dt_bias"][l])
    par = par.at[GDN_PAR_NORM, :].set(p["gdn_norm_g"][l])
    conv_w = jnp.concatenate([p["gdn_conv_w"][l], jnp.zeros((SUBLANES - CONV_W, CONV_DIM), F32)], axis=0)
    return dict(
        ffn1=(bf(p["ffn1_w1"]), bf(p["ffn1_w3"]), bf(p["ffn1_w2"]), row(p["ln1_g"]), row(p["ln1_b"])),
        ffn2=(bf(p["ffn2_w1"]), bf(p["ffn2_w3"]), bf(p["ffn2_w2"]), row(p["ln3_g"]), row(p["ln3_b"])),
        w_in=_regroup_w_in(p["w_in"][l]),
        wkvt=p["w_in"][l][:, _W_IN_K:_W_IN_K + 2 * FOX_DIM].T.astype(BF16),
        sgu_ln=(row(p["sgu_ln_g"]), row(p["sgu_ln_b"])),
        sgu_w=p["sgu_w"][l], sgu_b=p["sgu_b"][l],
        f_bias=_lane_row(p["fox_f_bias"][l], SMALL_F),
        conv_w=conv_w, gdn_par=par,
        merge=(row(p["gate_bias"]), bf(p["proj_a"]), bf(p["proj_b"]), bf(p["proj_c"]), bf(p["w_out"]),
               row(p["ln2_g"]), row(p["ln2_b"])),
    )


def _mixer_front(x, lp, n_seq, seq_rows, tm, conv0, s0, long_seq):
    if long_seq:
        za, zq, zz, qb, k, v, kb, vb, zg, zs = _in_proj(x, lp["w_in"], lp["wkvt"], n_seq, seq_rows, tm)
    else:
        za, zq, zz, qb, k, v, zg, zs = _in_proj(x, lp["w_in"], None, n_seq, seq_rows, tm)
        kb = vb = None
    emit_vn = not long_seq
    if seq_rows >= SGU_CHUNK:
        sgu_w, sgu_b, span = lp["sgu_w"], lp["sgu_b"], SGU_CHUNK
    else:
        rep = SGU_CHUNK // seq_rows
        sgu_w = jnp.tile(lp["sgu_w"][:, :seq_rows, :seq_rows], (1, rep, rep))
        sgu_b = jnp.tile(lp["sgu_b"][:, :seq_rows], (1, rep))
        span = seq_rows
    sgu_out = _sgu(za, *lp["sgu_ln"], sgu_w, sgu_b[:, :, None], span, tm, emit_vn)
    lf, ccol, crow = _fox_prep(zs, lp["f_bias"], seq_rows)
    rows_in = min(seq_rows, 4 * GDN_CHUNK)
    rows = max(rows_in, GDN_CHUNK)
    yb, s_fin = _gdn(zq, zs, zz, conv0, s0, lp["conv_w"], lp["gdn_par"], n_seq, seq_rows, rows_in, rows)
    return dict(ya=sgu_out[0], vn=sgu_out[1] if emit_vn else None, yb=yb, s_fin=s_fin, zq=zq, zg=zg,
                qb=qb, k=k, v=v, kb=kb, vb=vb, lf=lf, ccol=ccol, crow=crow)


def kernel(x_prompt, x_sample, cache_k, cache_v, cache_logf, page_table, state_conv, state_gdn,
           ln1_g, ln1_b, ffn1_w1, ffn1_w3, ffn1_w2, w_in, sgu_ln_g, sgu_ln_b, sgu_w, sgu_b,
           gdn_conv_w, gdn_a_log, gdn_dt_bias, gdn_norm_g, fox_f_bias, gate_bias,
           proj_a, proj_b, proj_c, w_out, ln2_g, ln2_b, ffn2_w1, ffn2_w3, ffn2_w2, ln3_g, ln3_b):
    p = dict(ln1_g=ln1_g, ln1_b=ln1_b, ffn1_w1=ffn1_w1, ffn1_w3=ffn1_w3, ffn1_w2=ffn1_w2, w_in=w_in,
             sgu_ln_g=sgu_ln_g, sgu_ln_b=sgu_ln_b, sgu_w=sgu_w, sgu_b=sgu_b, gdn_conv_w=gdn_conv_w,
             gdn_a_log=gdn_a_log, gdn_dt_bias=gdn_dt_bias, gdn_norm_g=gdn_norm_g, fox_f_bias=fox_f_bias,
             gate_bias=gate_bias, proj_a=proj_a, proj_b=proj_b, proj_c=proj_c, w_out=w_out, ln2_g=ln2_g,
             ln2_b=ln2_b, ffn2_w1=ffn2_w1, ffn2_w3=ffn2_w3, ffn2_w2=ffn2_w2, ln3_g=ln3_g, ln3_b=ln3_b)
    bp, sp, _ = x_prompt.shape
    db, ds, _ = x_sample.shape
    tp, ts = bp * sp, db * ds
    n_pool = cache_k.shape[1]
    tm_p = 512 if tp % 512 == 0 else 256
    tm_s = ts
    assert ts == PREP_ROWS and sp % FOX_TQ == 0 and ds == SUBLANES and cache_k.shape[2] == PAGE_SIZE

    yp = x_prompt.reshape(tp, D_MODEL)
    ys = x_sample.reshape(ts, D_MODEL)
    outs = [[] for _ in range(11)]
    conv_pad = jnp.zeros((SUBLANES - (CONV_W - 1), CONV_DIM), F32)
    ck = jnp.transpose(cache_k, (0, 1, 3, 4, 2)).reshape(DEPTH, n_pool, FOX_DIM, PAGE_SIZE)
    cv = jnp.transpose(cache_v, (0, 1, 3, 4, 2)).reshape(DEPTH, n_pool, FOX_DIM, PAGE_SIZE)
    lfc = jnp.swapaxes(cache_logf, 2, 3)
    for l in range(DEPTH):
        lp = _layer_params(l, p)

        yp = _ffn_ln(yp, *lp["ffn1"], tm_p)
        conv0 = jnp.zeros((bp, SUBLANES, CONV_DIM), F32)
        s0 = jnp.zeros((bp, GDN_HEADS, GDN_DK, GDN_DV), F32)
        f = _mixer_front(yp, lp, bp, sp, FOX_TK, conv0, s0, True)
        nkv = sp // FOX_TK
        crow = f["crow"].reshape(bp, nkv, FOX_HEADS, FOX_TK)
        yc = _fox_prompt(f["qb"], f["kb"], f["vb"], f["ccol"], crow, bp, sp)
        yp = _merge(f["ya"], f["yb"], yc, f["zg"], yp, *lp["merge"], tm_p)
        yp = _ffn_ln(yp, *lp["ffn2"], tm_p)
        to_bshd = lambda a: jnp.transpose(a.reshape(bp, FOX_HEADS, FOX_HD, sp), (0, 3, 1, 2))
        outs[0].append(to_bshd(f["k"]))
        outs[1].append(to_bshd(f["v"]))
        outs[2].append(jnp.transpose(f["lf"], (0, 2, 1)))
        outs[3].append(f["zq"].reshape(bp, sp, CONV_DIM)[:, sp - (CONV_W - 1):])
        outs[4].append(f["s_fin"])

        ys = _ffn_ln(ys, *lp["ffn1"], tm_s)
        conv0 = jnp.concatenate([jnp.broadcast_to(conv_pad, (db,) + conv_pad.shape), state_conv[l]], axis=1)
        f = _mixer_front(ys, lp, db, ds, tm_s, conv0, state_gdn[l], False)
        crow = jnp.pad(f["crow"][0].reshape(FOX_HEADS, db, ds).transpose(1, 0, 2), ((0, 0), (0, 0), (0, LANES - ds)))
        yc = _fox_sample(page_table, f["qb"], f["k"], f["v"], f["ccol"], crow, lfc, ck, cv, l)
        ys = _merge(f["ya"], f["yb"], yc, f["zg"], ys, *lp["merge"], tm_s)
        ys = _ffn_ln(ys, *lp["ffn2"], tm_s)
        outs[5].append(f["k"].reshape(db, ds, FOX_HEADS, FOX_HD))
        outs[6].append(f["v"].reshape(db, ds, FOX_HEADS, FOX_HD))
        outs[7].append(f["lf"][:, SMALL_F:SMALL_F + FOX_HEADS].reshape(db, ds, FOX_HEADS))
        outs[8].append(f["zq"].reshape(db, ds, CONV_DIM)[:, ds - (CONV_W - 1):])
        outs[9].append(f["s_fin"])
        outs[10].append(f["vn"].reshape(db, ds, D_A))

    return (yp.reshape(bp, sp, D_MODEL), ys.reshape(db, ds, D_MODEL)) + tuple(jnp.stack(o) for o in outs)
```

```python
import functools

import jax
import jax.numpy as jnp
from jax import lax
from jax.experimental import pallas as pl
from jax.experimental.pallas import tpu as pltpu

F32 = jnp.float32
BF16 = jnp.bfloat16

D_MODEL = 1024
DEPTH = 2
PAGE_SIZE = 128
D_FF = ((8 * D_MODEL // 3 + 127) // 128) * 128
SGU_CHUNK = 128
SGU_GROUPS = 4
D_A = D_MODEL // 2
GDN_HEADS = 4
GDN_DK = 128
GDN_DV = 128
GDN_CHUNK = 64
CONV_W = 4
CONV_DIM = GDN_HEADS * (2 * GDN_DK + GDN_DV)
FOX_HEADS = 8
FOX_HD = 64
FOX_DIM = FOX_HEADS * FOX_HD
N_BRANCH = 3
ALPHA = (2.0 * DEPTH) ** 0.25
LN_EPS = 1e-5
NORM_EPS = 1e-6

LANES = 128
SUBLANES = 8
VMEM_LIMIT_BYTES = 56 * 1024 * 1024

SEG_A = 2 * D_A
SEG_BQKV = CONV_DIM
SEG_BZ = GDN_HEADS * GDN_DV
SEG_SMALL = LANES
SMALL_F = 0
SMALL_BETA = FOX_HEADS
SMALL_A = FOX_HEADS + GDN_HEADS

HIGHEST = lax.Precision.HIGHEST


def _cparams(*sem):
    return pltpu.CompilerParams(dimension_semantics=sem, vmem_limit_bytes=VMEM_LIMIT_BYTES)


def _const_spec(shape):
    nd = len(shape)
    return pl.BlockSpec(shape, lambda *_: (0,) * nd, pipeline_mode=pl.Buffered(1))


def _layer_spec(shape, layer):
    nd = len(shape) - 1
    return pl.BlockSpec((None,) + tuple(shape[1:]), lambda *_: (layer,) + (0,) * nd, pipeline_mode=pl.Buffered(1))


def _layer_norm(x, g, b):
    mu = jnp.mean(x, axis=-1, keepdims=True)
    xc = x - mu
    var = jnp.mean(xc * xc, axis=-1, keepdims=True)
    return xc * lax.rsqrt(var + LN_EPS) * g + b


def _softplus(x):
    return jnp.maximum(x, 0.0) + jnp.log1p(jnp.exp(-jnp.abs(x)))


def _log_sigmoid(x):
    return jnp.minimum(x, 0.0) - jnp.log1p(jnp.exp(-jnp.abs(x)))


def _dot(a, b):
    return jnp.dot(a, b, preferred_element_type=F32)


def _dot_nt(a, b):
    return lax.dot_general(a, b, (((1,), (1,)), ((), ())), preferred_element_type=F32)


def _bmm(a, b):
    return lax.dot_general(a.astype(BF16), b.astype(BF16), (((2,), (1,)), ((0,), (0,))), preferred_element_type=F32)


def _bmm_nt(a, b):
    return lax.dot_general(a.astype(BF16), b.astype(BF16), (((2,), (2,)), ((0,), (0,))), preferred_element_type=F32)


def _iota(shape, dim):
    return lax.broadcasted_iota(jnp.int32, shape, dim)


FFN_SPLITS = 2


def _ffn_ln_kernel(x_ref, w1_ref, w3_ref, w2_ref, g_ref, b_ref, o_ref):
    x = x_ref[...]
    xb = x.astype(BF16)
    fc = D_FF // FFN_SPLITS
    y = None
    for c in range(FFN_SPLITS):
        h1 = _dot(xb, w1_ref[:, c * fc:(c + 1) * fc])
        h3 = _dot(xb, w3_ref[:, c * fc:(c + 1) * fc])
        a = (h1 * jax.nn.sigmoid(h1) * h3).astype(BF16)
        part = _dot(a, w2_ref[c * fc:(c + 1) * fc, :])
        y = part if y is None else y + part
    o_ref[...] = _layer_norm(ALPHA * x + 0.5 * y, g_ref[...], b_ref[...])


def _ffn_ln(x, w1, w3, w2, g, b, layer, tm):
    t = x.shape[0]
    row = pl.BlockSpec((tm, D_MODEL), lambda i: (i, 0))
    return pl.pallas_call(
        _ffn_ln_kernel,
        grid=(t // tm,),
        in_specs=[row] + [_layer_spec(a.shape, layer) for a in (w1, w3, w2, g, b)],
        out_specs=row,
        out_shape=jax.ShapeDtypeStruct((t, D_MODEL), F32),
        compiler_params=_cparams("parallel"),
        name="ffn_ln",
    )(x, w1, w3, w2, g, b)


_IN_SEGS = (SEG_A, SEG_BQKV, SEG_BZ, FOX_DIM, FOX_DIM, FOX_DIM, N_BRANCH * D_MODEL, SEG_SMALL)
_IN_OFFS = tuple(sum(_IN_SEGS[:i]) for i in range(len(_IN_SEGS)))


def _in_proj_kernel(x_ref, w_ref, *refs, kv_transposed):
    xb = x_ref[...].astype(BF16)

    def seg(i):
        return _dot(xb, w_ref[:, _IN_OFFS[i]:_IN_OFFS[i] + _IN_SEGS[i]])

    if kv_transposed:
        wkvt_ref, za_ref, zq_ref, zz_ref, q_ref, kt_ref, vt_ref, ktb_ref, vtb_ref, zg_ref, zs_ref = refs
        kt = _dot_nt(wkvt_ref[0:FOX_DIM, :], xb)
        kt_ref[...] = kt
        ktb_ref[...] = kt.astype(BF16)
        vt = _dot_nt(wkvt_ref[FOX_DIM:2 * FOX_DIM, :], xb)
        vt_ref[...] = vt
        vtb_ref[...] = vt.astype(BF16)
    else:
        za_ref, zq_ref, zz_ref, q_ref, k_ref, v_ref, zg_ref, zs_ref = refs
        k_ref[...] = seg(4)
        v_ref[...] = seg(5)
    za_ref[...] = seg(0)
    zq_ref[...] = seg(1)
    zz_ref[...] = seg(2)
    q_ref[...] = (seg(3) * (FOX_HD ** -0.5)).astype(q_ref.dtype)
    zg_ref[...] = seg(6)
    zs_ref[...] = seg(7)


def _in_proj(x, w, wkvt, n_seq, seq_rows, tm):
    t = x.shape[0]
    row = lambda n: pl.BlockSpec((tm, n), lambda i: (i, 0))
    sds = jax.ShapeDtypeStruct
    head = [(row(SEG_A), sds((t, SEG_A), F32)), (row(SEG_BQKV), sds((t, SEG_BQKV), F32)),
            (row(SEG_BZ), sds((t, SEG_BZ), F32)), (row(FOX_DIM), sds((t, FOX_DIM), F32 if wkvt is None else BF16))]
    tail = [(row(N_BRANCH * D_MODEL), sds((t, N_BRANCH * D_MODEL), F32)), (row(SEG_SMALL), sds((t, SEG_SMALL), F32))]
    in_specs = [row(D_MODEL), _const_spec(w.shape)]
    operands = [x, w]
    if wkvt is None:
        kv = [(row(FOX_DIM), sds((t, FOX_DIM), F32))] * 2
    else:
        nt = seq_rows // tm
        f32_spec = pl.BlockSpec((None, FOX_DIM, tm), lambda i: (i // nt, 0, i % nt))
        bf_spec = pl.BlockSpec((None, None, FOX_DIM, tm), lambda i: (i // nt, i % nt, 0, 0))
        kv = [(f32_spec, sds((n_seq, FOX_DIM, seq_rows), F32))] * 2 + [(bf_spec, sds((n_seq, nt, FOX_DIM, tm), BF16))] * 2
        in_specs.append(_const_spec(wkvt.shape))
        operands.append(wkvt)
    outs = head + kv + tail
    return pl.pallas_call(
        functools.partial(_in_proj_kernel, kv_transposed=wkvt is not None),
        grid=(t // tm,),
        in_specs=in_specs,
        out_specs=[o[0] for o in outs],
        out_shape=[o[1] for o in outs],
        compiler_params=_cparams("parallel"),
        name="in_proj",
    )(*operands)


def _sgu_kernel(za_ref, g_ref, b_ref, w_ref, bs_ref, ya_ref, *maybe_vn_ref, seq_rows, tm):
    u = jax.nn.gelu(za_ref[:, :D_A], approximate=True)
    vn = _layer_norm(jax.nn.gelu(za_ref[:, D_A:], approximate=True), g_ref[...], b_ref[...])
    if maybe_vn_ref:
        maybe_vn_ref[0][...] = vn
    r = _iota((SGU_CHUNK, SGU_CHUNK), 0)
    c = _iota((SGU_CHUNK, SGU_CHUNK), 1)
    keep = (r >= c) & ((r // seq_rows) == (c // seq_rows))
    cg = D_A // SGU_GROUPS
    for g in range(SGU_GROUPS):
        wg = jnp.where(keep, w_ref[g], 0.0).astype(BF16)
        bg = bs_ref[g]
        for n in range(tm // SGU_CHUNK):
            rows = slice(n * SGU_CHUNK, (n + 1) * SGU_CHUNK)
            cols = slice(g * cg, (g + 1) * cg)
            mix = _dot(wg, vn[rows, cols].astype(BF16)) + bg
            ya_ref[rows, cols] = u[rows, cols] * mix


def _sgu(za, ln_g, ln_b, w, bs, seq_rows, tm, emit_vn):
    t = za.shape[0]
    out_spec = pl.BlockSpec((tm, D_A), lambda i: (i, 0))
    out_shape = jax.ShapeDtypeStruct((t, D_A), F32)
    return pl.pallas_call(
        functools.partial(_sgu_kernel, seq_rows=seq_rows, tm=tm),
        grid=(t // tm,),
        in_specs=[pl.BlockSpec((tm, SEG_A), lambda i: (i, 0)), _const_spec(ln_g.shape), _const_spec(ln_b.shape),
                  _const_spec(w.shape), _const_spec(bs.shape)],
        out_specs=[out_spec, out_spec] if emit_vn else [out_spec],
        out_shape=[out_shape, out_shape] if emit_vn else [out_shape],
        compiler_params=_cparams("parallel"),
        name="sgu",
    )(za, ln_g, ln_b, w, bs)


PREP_ROWS = 256


def _fox_prep_kernel(zs_ref, fb_ref, lf_ref, crow_ref, carry_ref, *, seq_rows):
    lf = _log_sigmoid(zs_ref[...] + fb_ref[...])
    if seq_rows > PREP_ROWS:
        lf_ref[...] = lf.T[0:FOX_HEADS, :]
    else:
        lf_ref[...] = lf
    r = _iota((PREP_ROWS, PREP_ROWS), 0)
    c = _iota((PREP_ROWS, PREP_ROWS), 1)
    span = min(seq_rows, PREP_ROWS)
    tri = jnp.where((r >= c) & ((r // span) == (c // span)), 1.0, 0.0).astype(F32)
    csum = jnp.dot(tri, lf, precision=HIGHEST, preferred_element_type=F32)
    if seq_rows > PREP_ROWS:
        blocks_per_seq = seq_rows // PREP_ROWS

        @pl.when(pl.program_id(0) % blocks_per_seq == 0)
        def _():
            carry_ref[...] = jnp.zeros_like(carry_ref)

        csum = csum + carry_ref[0:1, :]
        carry_ref[...] = jnp.broadcast_to(csum[PREP_ROWS - 1:PREP_ROWS, :], carry_ref.shape)
    crow_ref[0] = csum.T[0:FOX_HEADS, :]


def _fox_prep(zs, f_bias_row, seq_rows):
    t = zs.shape[0]
    nb = t // PREP_ROWS
    row = pl.BlockSpec((PREP_ROWS, LANES), lambda i: (i, 0))
    if seq_rows > PREP_ROWS:
        bps = seq_rows // PREP_ROWS
        lf_spec = pl.BlockSpec((None, FOX_HEADS, PREP_ROWS), lambda i: (i // bps, 0, i % bps))
        lf_shape = jax.ShapeDtypeStruct((t // seq_rows, FOX_HEADS, seq_rows), F32)
    else:
        lf_spec, lf_shape = row, jax.ShapeDtypeStruct((t, LANES), F32)
    return pl.pallas_call(
        functools.partial(_fox_prep_kernel, seq_rows=seq_rows),
        grid=(nb,),
        in_specs=[row, _const_spec(f_bias_row.shape)],
        out_specs=[lf_spec, pl.BlockSpec((1, FOX_HEADS, PREP_ROWS), lambda i: (i, 0, 0))],
        out_shape=[lf_shape, jax.ShapeDtypeStruct((nb, FOX_HEADS, PREP_ROWS), F32)],
        scratch_shapes=[pltpu.VMEM((SUBLANES, LANES), F32)],
        compiler_params=_cparams("arbitrary"),
        name="fox_prep",
    )(zs, f_bias_row)


GDN_PAR_ALOG = 0
GDN_PAR_DTB = 1
GDN_PAR_NORM = 2


def _unit_lower_inverse_minus_identity(a):
    n = GDN_CHUNK
    r = _iota((1, n, n), 1)
    c = _iota((1, n, n), 2)
    same16 = (r // 16) == (c // 16)
    same32 = (r // 32) == (c // 32)
    ad = jnp.where(same16, a, 0.0)
    t = -ad
    p = _bmm(ad, ad)
    t = t + p + _bmm(t, p)
    p = _bmm(p, p)
    t = t + p + _bmm(t, p)
    p = _bmm(p, p)
    t = t + p + _bmm(t, p)
    for off in (jnp.where(same32 & ~same16, a, 0.0), jnp.where(~same32, a, 0.0)):
        m = off + _bmm(t, off)
        t = t - m - _bmm(m, t)
    return t


def _gdn_kernel(zq_ref, zs_ref, zz_ref, conv0_ref, s0_ref, cw_ref, par_ref, y_ref, sfin_ref, xp_scr, s_scr,
                *, ns, nc, rows_in, blocks_per_seq):
    n = GDN_CHUNK
    rows = nc * n
    nch = ns * nc
    blk = pl.program_id(1)

    @pl.when(blk == 0)
    def _():
        xp_scr[:, 0:SUBLANES, :] = conv0_ref[...]
        s_scr[...] = s0_ref[...].reshape(ns * GDN_HEADS, GDN_DK, GDN_DV)

    @pl.when(blk > 0)
    def _():
        xp_scr[:, 0:SUBLANES, :] = xp_scr[:, rows:rows + SUBLANES, :]

    xp_scr[:, SUBLANES:SUBLANES + rows_in, :] = zq_ref[...].reshape(ns, rows_in, CONV_DIM)
    if rows_in < rows:
        xp_scr[:, SUBLANES + rows_in:SUBLANES + rows, :] = jnp.zeros((ns, rows - rows_in, CONV_DIM), F32)

    base = SUBLANES - (CONV_W - 1)
    yc = xp_scr[:, base:base + rows, :] * cw_ref[0:1, :]
    for i in range(1, CONV_W):
        yc = yc + xp_scr[:, base + i:base + i + rows, :] * cw_ref[i:i + 1, :]
    yc = (yc * jax.nn.sigmoid(yc)).reshape(ns * rows, CONV_DIM)

    zs = zs_ref[...].reshape(ns, rows_in, LANES)
    if rows_in < rows:
        zs = jnp.concatenate([zs, jnp.zeros((ns, rows - rows_in, LANES), F32)], axis=1)
    valid = _iota((ns, rows, LANES), 1) < rows_in
    sig = jnp.where(valid, jax.nn.sigmoid(zs), 0.0).reshape(ns * rows, LANES)
    gl = -jnp.exp(par_ref[GDN_PAR_ALOG:GDN_PAR_ALOG + 1, :]) * _softplus(zs + par_ref[GDN_PAR_DTB:GDN_PAR_DTB + 1, :])
    gl = jnp.where(valid, gl, 0.0).reshape(ns * rows, LANES)

    rr = ns * rows
    r = _iota((rr, rr), 0)
    c = _iota((rr, rr), 1)
    tri = jnp.where((r >= c) & ((r // n) == (c // n)), 1.0, 0.0).astype(F32)
    gc = jnp.dot(tri, gl, precision=HIGHEST, preferred_element_type=F32)
    gct = gc.T

    order = [(cc, s, h) for cc in range(nc) for s in range(ns) for h in range(GDN_HEADS)]
    nb = len(order)

    def rows_of(cc, s):
        return slice(s * rows + cc * n, s * rows + (cc + 1) * n)

    def cols3(x, off, width):
        return jnp.stack([x[rows_of(cc, s), off + h * width:off + (h + 1) * width] for cc, s, h in order], axis=0)

    q = cols3(yc, 0, GDN_DK)
    k = cols3(yc, GDN_HEADS * GDN_DK, GDN_DK)
    v = cols3(yc, 2 * GDN_HEADS * GDN_DK, GDN_DV)
    beta = cols3(sig, SMALL_BETA, 1)
    gcol = cols3(gc, SMALL_A, 1)
    grow = jnp.stack([gct[SMALL_A + h:SMALL_A + h + 1, rows_of(cc, s)] for cc, s, h in order], axis=0)
    glast = gcol[:, n - 1:n, :]

    q = q * lax.rsqrt(jnp.sum(q * q, axis=-1, keepdims=True) + NORM_EPS) * (GDN_DK ** -0.5)
    k = k * lax.rsqrt(jnp.sum(k * k, axis=-1, keepdims=True) + NORM_EPS)
    ri = _iota((1, n, n), 1)
    ci = _iota((1, n, n), 2)
    decay = jnp.exp(jnp.where(ri >= ci, gcol - grow, -jnp.inf))
    egc = jnp.exp(gcol)
    qk_kk = _bmm_nt(jnp.concatenate([q, k], axis=1), k)
    qk = qk_kk[:, :n] * decay
    a = jnp.where(ri > ci, beta * qk_kk[:, n:] * decay, 0.0)
    tinv = _unit_lower_inverse_minus_identity(a)
    rhs = jnp.concatenate([v * beta, k * (beta * egc)], axis=-1)
    sol = rhs + _bmm(tinv, rhs)
    w_val = sol[:, :, :GDN_DV]
    k_cum = sol[:, :, GDN_DV:]
    q_dec = q * egc
    k_tail_t = jnp.swapaxes(k * jnp.exp(glast - gcol), 1, 2)
    g_last = jnp.exp(glast)

    norm_g = par_ref[GDN_PAR_NORM:GDN_PAR_NORM + 1, :]
    per = ns * GDN_HEADS
    for cc in range(nc):
        sl = slice(cc * per, (cc + 1) * per)
        s_old = s_scr[...]
        v_new = w_val[sl] - _bmm(k_cum[sl], s_old)
        o = _bmm(q_dec[sl], s_old) + _bmm(qk[sl], v_new)
        s_scr[...] = s_old * g_last[sl] + _bmm(k_tail_t[sl], v_new)
        nv = min(rows_in - cc * n, n)
        for s in range(ns):
            for h in range(GDN_HEADS):
                r0 = s * rows_in + cc * n
                cols = slice(h * GDN_DV, (h + 1) * GDN_DV)
                zz = zz_ref[r0:r0 + nv, cols]
                ov = o[s * GDN_HEADS + h, :nv]
                gate = zz * jax.nn.sigmoid(zz)
                y_ref[r0:r0 + nv, cols] = (ov * lax.rsqrt(jnp.mean(ov * ov, axis=-1, keepdims=True) + NORM_EPS)
                                           * norm_g * gate)

    @pl.when(blk == blocks_per_seq - 1)
    def _():
        sfin_ref[...] = s_scr[...].reshape(ns, GDN_HEADS, GDN_DK, GDN_DV)


def _gdn(zq, zs, zz, conv0, s0, conv_w, par, n_seq, seq_rows, ns, nc, rows_in):
    t = zq.shape[0]
    bps = seq_rows // rows_in
    rows = nc * GDN_CHUNK

    def row(ncol):
        return pl.BlockSpec((ns * rows_in, ncol), lambda b, j: (b * bps + j, 0))

    state = pl.BlockSpec((ns, GDN_HEADS, GDN_DK, GDN_DV), lambda b, j: (b, 0, 0, 0))
    return pl.pallas_call(
        functools.partial(_gdn_kernel, ns=ns, nc=nc, rows_in=rows_in, blocks_per_seq=bps),
        grid=(n_seq // ns, bps),
        in_specs=[row(CONV_DIM), row(LANES), row(SEG_BZ),
                  pl.BlockSpec((ns, SUBLANES, CONV_DIM), lambda b, j: (b, 0, 0)), state,
                  _const_spec(conv_w.shape), _const_spec(par.shape)],
        out_specs=[row(SEG_BZ), state],
        out_shape=[jax.ShapeDtypeStruct((t, SEG_BZ), F32),
                   jax.ShapeDtypeStruct((n_seq, GDN_HEADS, GDN_DK, GDN_DV), F32)],
        scratch_shapes=[pltpu.VMEM((ns, rows + 2 * SUBLANES, CONV_DIM), F32),
                        pltpu.VMEM((ns * GDN_HEADS, GDN_DK, GDN_DV), F32)],
        compiler_params=_cparams("parallel", "arbitrary"),
        name="gdn",
    )(zq, zs, zz, conv0, s0, conv_w, par)


FOX_TQ = 512
FOX_SUB = 128
FOX_TK = PREP_ROWS


def _fox_prompt_kernel(qb_ref, kt_ref, vt_ref, crow_ref, o_ref, *, nq):
    hp = pl.program_id(1)
    nsub = FOX_TQ // FOX_SUB
    below = _iota((FOX_SUB, FOX_TK), 0) - _iota((FOX_SUB, FOX_TK), 1)
    ones = jnp.ones((FOX_HD, FOX_TK), BF16)

    def q_block(qi, _):
        r0 = pl.multiple_of(qi * FOX_TQ, FOX_TQ)
        qs = [[qb_ref[pl.ds(r0 + sb * FOX_SUB, FOX_SUB), e * FOX_HD:(e + 1) * FOX_HD] for sb in range(nsub)]
              for e in range(2)]

        def tile(j, carry, diag):
            out = list(carry)
            for e in range(2):
                kt = kt_ref[j, e * FOX_HD:(e + 1) * FOX_HD, :]
                vt1 = jnp.concatenate([vt_ref[j, e * FOX_HD:(e + 1) * FOX_HD, :], ones], axis=0)
                nrow = -crow_ref[j, pl.ds(2 * hp + e, 1), :]
                for sb in range(nsub):
                    shift = sb * FOX_SUB - (0 if diag is None else diag * FOX_TK)
                    if diag is not None and shift + FOX_SUB - 1 < 0:
                        continue
                    m, acc = carry[e * nsub + sb]
                    s = _dot(qs[e][sb], kt) + nrow
                    if diag is not None and shift < FOX_TK - 1:
                        s = jnp.where(below + shift >= 0, s, -jnp.inf)
                    m_new = jnp.maximum(m, jnp.max(s, axis=-1, keepdims=True))
                    p = jnp.exp(s - m_new)
                    acc = jnp.exp(m - m_new) * acc + _dot_nt(p.astype(BF16), vt1)
                    out[e * nsub + sb] = (m_new, acc)
            return tuple(out)

        init = tuple((jnp.full((FOX_SUB, 1), -jnp.inf, F32), jnp.zeros((FOX_SUB, 2 * FOX_HD), F32))
                     for _ in range(2 * nsub))
        jd = qi * (FOX_TQ // FOX_TK)
        carry = lax.fori_loop(0, jd, lambda j, cr: tile(j, cr, None), init)
        for d in range(FOX_TQ // FOX_TK):
            carry = tile(jd + d, carry, d)
        for sb in range(nsub):
            acc = [carry[e * nsub + sb][1] for e in range(2)]
            o_ref[pl.ds(r0 + sb * FOX_SUB, FOX_SUB), :] = jnp.concatenate(
                [a[:, :FOX_HD] / a[:, FOX_HD:] for a in acc], axis=-1)
        return 0

    lax.fori_loop(0, nq, q_block, 0)


def _fox_prompt(qb, ktb, vtb, crow, n_seq, seq_rows):
    t = qb.shape[0]
    nkv = seq_rows // FOX_TK
    qspec = pl.BlockSpec((seq_rows, LANES), lambda b, h: (b, h))
    kvspec = pl.BlockSpec((None, nkv, LANES, FOX_TK), lambda b, h: (b, 0, h, 0))
    return pl.pallas_call(
        functools.partial(_fox_prompt_kernel, nq=seq_rows // FOX_TQ),
        grid=(n_seq, FOX_HEADS // 2),
        in_specs=[qspec, kvspec, kvspec,
                  pl.BlockSpec((None, nkv, FOX_HEADS, FOX_TK), lambda b, h: (b, 0, 0, 0))],
        out_specs=qspec,
        out_shape=jax.ShapeDtypeStruct((t, FOX_DIM), F32),
        compiler_params=_cparams("parallel", "parallel"),
        name="fox_prompt",
    )(qb, ktb, vtb, crow)


FS_GROUP = 8
FS_SLOTS = 3
FS_ROWS = FOX_HEADS * SUBLANES


def _fox_sample_kernel(pt_ref, q_ref, kn_ref, vn_ref, crow_ref, lfc_ref, ck_ref, cv_ref, o_ref,
                       kbuf, vbuf, lbuf, sem, *, n_seq, n_pages, layer):
    n_groups = n_pages // FS_GROUP
    total = n_seq * n_groups
    gw = FS_GROUP * PAGE_SIZE

    def copies(f, slot):
        b = f // n_groups
        gi = f % n_groups
        out = []
        for u in range(FS_GROUP):
            page = pt_ref[b, n_pages - 1 - (gi * FS_GROUP + u)]
            lanes = slice(u * PAGE_SIZE, (u + 1) * PAGE_SIZE)
            out.append(pltpu.make_async_copy(ck_ref.at[layer, page], kbuf.at[slot, :, lanes], sem.at[0, slot]))
            out.append(pltpu.make_async_copy(cv_ref.at[layer, page], vbuf.at[slot, :, lanes], sem.at[1, slot]))
            out.append(pltpu.make_async_copy(lfc_ref.at[layer, page],
                                             lbuf.at[slot, u * FOX_HEADS:(u + 1) * FOX_HEADS, :], sem.at[2, slot]))
        return out

    for f0 in range(FS_SLOTS - 1):
        for cp in copies(f0, f0):
            cp.start()

    rr = _iota((FS_ROWS, FOX_DIM), 0)
    cc = _iota((FS_ROWS, FOX_DIM), 1)
    own = (rr // SUBLANES) == (cc // FOX_HD)
    pr = _iota((PAGE_SIZE, PAGE_SIZE), 0)
    pc = _iota((PAGE_SIZE, PAGE_SIZE), 1)
    later = jnp.where(pr > pc, 1.0, 0.0).astype(F32)
    tok = _iota((FS_ROWS, PAGE_SIZE), 0) % SUBLANES
    pos = _iota((FS_ROWS, PAGE_SIZE), 1)
    hl = _iota((SUBLANES, FOX_DIM), 1) // FOX_HD
    pad = jnp.zeros((PAGE_SIZE - SUBLANES, FOX_DIM), F32)

    def rows8(x):
        return jnp.concatenate([jnp.broadcast_to(x[h:h + 1, :], (SUBLANES, x.shape[1])) for h in range(FOX_HEADS)], axis=0)

    def update(carry, s, vbf, v_transposed):
        m, l, acc = carry
        m_new = jnp.maximum(m, jnp.max(s, axis=-1, keepdims=True))
        alpha = jnp.exp(m - m_new)
        p = jnp.exp(s - m_new)
        l = alpha * l + jnp.sum(p, axis=-1, keepdims=True)
        pv = _dot_nt(p.astype(BF16), vbf) if v_transposed else _dot(p.astype(BF16), vbf)
        return m_new, l, alpha * acc + pv

    def sequence(b, _):
        qrep = jnp.concatenate([q_ref[b].astype(BF16)] * FOX_HEADS, axis=0)
        qbd = jnp.where(own, qrep, jnp.zeros_like(qrep))

        def group(gi, carry):
            m, l, acc, tail = carry
            f = b * n_groups + gi
            slot = f % FS_SLOTS
            nxt = f + FS_SLOTS - 1

            @pl.when(nxt < total)
            def _():
                for cp in copies(nxt, nxt % FS_SLOTS):
                    cp.start()

            for cp in copies(f, slot):
                cp.wait()
            lf = lbuf[slot]
            suffix = jnp.dot(lf, later, precision=HIGHEST, preferred_element_type=F32)
            tot = jnp.sum(lf, axis=-1, keepdims=True)
            bias = []
            for u in range(FS_GROUP):
                rows = slice(u * FOX_HEADS, (u + 1) * FOX_HEADS)
                bias.append(rows8(suffix[rows] + tail))
                tail = tail + tot[rows]
            s = _dot(qbd, kbuf[slot].astype(BF16)) + jnp.concatenate(bias, axis=1)
            m, l, acc = update((m, l, acc), s, vbuf[slot].astype(BF16), True)
            return m, l, acc, tail

        init = (jnp.full((FS_ROWS, 1), -jnp.inf, F32), jnp.zeros((FS_ROWS, 1), F32),
                jnp.zeros((FS_ROWS, FOX_DIM), F32), jnp.zeros((FOX_HEADS, 1), F32))
        m, l, acc, _ = lax.fori_loop(0, n_groups, group, init)

        kn = jnp.concatenate([kn_ref[b], pad], axis=0).astype(BF16)
        vn = jnp.concatenate([vn_ref[b], pad], axis=0).astype(BF16)
        s = jnp.where(pos <= tok, _dot_nt(qbd, kn) - rows8(crow_ref[b]), -jnp.inf)
        m, l, acc = update((m, l, acc), s, vn, False)

        o = acc / l
        out = jnp.zeros((SUBLANES, FOX_DIM), F32)
        for h in range(FOX_HEADS):
            out = jnp.where(hl == h, o[h * SUBLANES:(h + 1) * SUBLANES, :], out)
        o_ref[b] = out
        return 0

    lax.fori_loop(0, n_seq, sequence, 0)


def _fox_sample(page_table, q, kn, vn, crow, lfc, ck, cv, layer):
    n_seq, n_pages = page_table.shape
    gw = FS_GROUP * PAGE_SIZE
    whole = lambda a: pl.BlockSpec(a.shape, lambda i, pt: (0,) * a.ndim)
    any_spec = pl.BlockSpec(memory_space=pl.ANY)
    grid_spec = pltpu.PrefetchScalarGridSpec(
        num_scalar_prefetch=1,
        grid=(1,),
        in_specs=[whole(q), whole(kn), whole(vn), whole(crow), any_spec, any_spec, any_spec],
        out_specs=pl.BlockSpec((n_seq, SUBLANES, FOX_DIM), lambda i, pt: (0, 0, 0)),
        scratch_shapes=[pltpu.VMEM((FS_SLOTS, FOX_DIM, gw), F32),
                        pltpu.VMEM((FS_SLOTS, FOX_DIM, gw), F32),
                        pltpu.VMEM((FS_SLOTS, FS_GROUP * FOX_HEADS, PAGE_SIZE), F32),
                        pltpu.SemaphoreType.DMA((3, FS_SLOTS))],
    )
    return pl.pallas_call(
        functools.partial(_fox_sample_kernel, n_seq=n_seq, n_pages=n_pages, layer=layer),
        grid_spec=grid_spec,
        out_shape=jax.ShapeDtypeStruct((n_seq, SUBLANES, FOX_DIM), F32),
        compiler_params=_cparams("arbitrary"),
        name="fox_sample",
    )(page_table, q, kn, vn, crow, lfc, ck, cv)


def _merge_kernel(ya_ref, yb_ref, yc_ref, zg_ref, x_ref, gb_ref, pa_ref, pb_ref, pc_ref, wo_ref, g_ref, b_ref, o_ref):
    merged = None
    for i, (y_ref, p_ref) in enumerate(((ya_ref, pa_ref), (yb_ref, pb_ref), (yc_ref, pc_ref))):
        cols = slice(i * D_MODEL, (i + 1) * D_MODEL)
        gate = jax.nn.sigmoid(zg_ref[:, cols] + gb_ref[:, cols])
        term = gate * _dot(y_ref[...].astype(BF16), p_ref[...])
        merged = term if merged is None else merged + term
    mix = _dot(merged.astype(BF16), wo_ref[...])
    o_ref[...] = _layer_norm(ALPHA * x_ref[...] + mix, g_ref[...], b_ref[...])


def _merge(ya, yb, yc, zg, x, gate_bias, pa, pb, pc, wo, g, b, layer, tm):
    t = x.shape[0]
    row = lambda n: pl.BlockSpec((tm, n), lambda i: (i, 0))
    consts = (gate_bias, pa, pb, pc, wo, g, b)
    return pl.pallas_call(
        _merge_kernel,
        grid=(t // tm,),
        in_specs=[row(D_A), row(SEG_BZ), row(FOX_DIM), row(N_BRANCH * D_MODEL), row(D_MODEL)]
        + [_layer_spec(c.shape, layer) for c in consts],
        out_specs=row(D_MODEL),
        out_shape=jax.ShapeDtypeStruct((t, D_MODEL), F32),
        compiler_params=_cparams("parallel"),
        name="merge",
    )(ya, yb, yc, zg, x, *consts)


_W_IN_K = 2 * D_A + CONV_DIM + 2 * GDN_HEADS + SEG_BZ + FOX_DIM


def _regroup_w_in(w_in):
    o_bqkv = 2 * D_A
    o_beta = o_bqkv + CONV_DIM
    o_a = o_beta + GDN_HEADS
    o_z = o_a + GDN_HEADS
    o_cqkv = o_z + SEG_BZ
    o_f = o_cqkv + 3 * FOX_DIM
    o_gate = o_f + FOX_HEADS
    small = jnp.concatenate([w_in[:, o_f:o_gate], w_in[:, o_beta:o_a], w_in[:, o_a:o_z],
                             jnp.zeros((D_MODEL, SEG_SMALL - FOX_HEADS - 2 * GDN_HEADS), w_in.dtype)], axis=1)
    return jnp.concatenate([w_in[:, :o_beta], w_in[:, o_z:o_cqkv], w_in[:, o_cqkv:o_f], w_in[:, o_gate:], small],
                           axis=1).astype(BF16)


def _lane_row(vals, offset):
    return jnp.zeros((1, LANES), F32).at[0, offset:offset + vals.shape[0]].set(vals.astype(F32))


def _layer_params(l, p):
    par = jnp.zeros((SUBLANES, LANES), F32)
    par = par.at[GDN_PAR_ALOG, SMALL_A:SMALL_A + GDN_HEADS].set(p["gdn_a_log"][l])
    par = par.at[GDN_PAR_DTB, SMALL_A:SMALL_A + GDN_HEADS].set(p["gdn_dt_bias"][l])
    par = par.at[GDN_PAR_NORM, :].set(p["gdn_norm_g"][l])
    conv_w = jnp.concatenate([p["gdn_conv_w"][l], jnp.zeros((SUBLANES - CONV_W, CONV_DIM), F32)], axis=0)
    row = lambda a: a[l].reshape(1, -1)
    return dict(
        w_in=_regroup_w_in(p["w_in"][l]),
        wkvt=p["w_in"][l][:, _W_IN_K:_W_IN_K + 2 * FOX_DIM].T.astype(BF16),
        sgu_ln=(row(p["sgu_ln_g"]), row(p["sgu_ln_b"])),
        sgu_w=p["sgu_w"][l], sgu_b=p["sgu_b"][l],
        f_bias=_lane_row(p["fox_f_bias"][l], SMALL_F),
        conv_w=conv_w, gdn_par=par,
    )


def _mixer_front(x, lp, n_seq, seq_rows, tm, conv0, s0, long_seq):
    if long_seq:
        za, zq, zz, q, k, v, kb, vb, zg, zs = _in_proj(x, lp["w_in"], lp["wkvt"], n_seq, seq_rows, tm)
    else:
        za, zq, zz, q, k, v, zg, zs = _in_proj(x, lp["w_in"], None, n_seq, seq_rows, tm)
        kb = vb = None
    emit_vn = not long_seq
    if seq_rows >= SGU_CHUNK:
        sgu_w, sgu_b, span = lp["sgu_w"], lp["sgu_b"], SGU_CHUNK
    else:
        rep = SGU_CHUNK // seq_rows
        sgu_w = jnp.tile(lp["sgu_w"][:, :seq_rows, :seq_rows], (1, rep, rep))
        sgu_b = jnp.tile(lp["sgu_b"][:, :seq_rows], (1, rep))
        span = seq_rows
    sgu_out = _sgu(za, *lp["sgu_ln"], sgu_w, sgu_b[:, :, None], span, tm, emit_vn)
    lf, crow = _fox_prep(zs, lp["f_bias"], seq_rows)
    if long_seq:
        ns, nc, rows_in = 1, 4, 4 * GDN_CHUNK
    else:
        ns, nc, rows_in = 8, 1, seq_rows
    yb, s_fin = _gdn(zq, zs, zz, conv0, s0, lp["conv_w"], lp["gdn_par"], n_seq, seq_rows, ns, nc, rows_in)
    return dict(ya=sgu_out[0], vn=sgu_out[1] if emit_vn else None, yb=yb, s_fin=s_fin, zq=zq, zg=zg,
                q=q, k=k, v=v, kb=kb, vb=vb, lf=lf, crow=crow)


def kernel(x_prompt, x_sample, cache_k, cache_v, cache_logf, page_table, state_conv, state_gdn,
           ln1_g, ln1_b, ffn1_w1, ffn1_w3, ffn1_w2, w_in, sgu_ln_g, sgu_ln_b, sgu_w, sgu_b,
           gdn_conv_w, gdn_a_log, gdn_dt_bias, gdn_norm_g, fox_f_bias, gate_bias,
           proj_a, proj_b, proj_c, w_out, ln2_g, ln2_b, ffn2_w1, ffn2_w3, ffn2_w2, ln3_g, ln3_b):
    p = dict(w_in=w_in, sgu_ln_g=sgu_ln_g, sgu_ln_b=sgu_ln_b, sgu_w=sgu_w, sgu_b=sgu_b, gdn_conv_w=gdn_conv_w,
             gdn_a_log=gdn_a_log, gdn_dt_bias=gdn_dt_bias, gdn_norm_g=gdn_norm_g, fox_f_bias=fox_f_bias)
    bp, sp, _ = x_prompt.shape
    db, ds, _ = x_sample.shape
    tp, ts = bp * sp, db * ds
    n_pool = cache_k.shape[1]
    tm_p = 512 if tp % 512 == 0 else 256
    tm_s = ts
    assert ts == PREP_ROWS and sp % FOX_TQ == 0 and ds == SUBLANES and cache_k.shape[2] == PAGE_SIZE
    assert page_table.shape[1] % FS_GROUP == 0 and db % 8 == 0

    bf = lambda a: a.astype(BF16)
    rows3 = lambda a: a[:, None, :]
    ffn1 = (bf(ffn1_w1), bf(ffn1_w3), bf(ffn1_w2), rows3(ln1_g), rows3(ln1_b))
    ffn2 = (bf(ffn2_w1), bf(ffn2_w3), bf(ffn2_w2), rows3(ln3_g), rows3(ln3_b))
    mrg = (rows3(gate_bias), bf(proj_a), bf(proj_b), bf(proj_c), bf(w_out), rows3(ln2_g), rows3(ln2_b))

    yp = x_prompt.reshape(tp, D_MODEL)
    ys = x_sample.reshape(ts, D_MODEL)
    outs = [[] for _ in range(11)]
    conv_pad = jnp.zeros((SUBLANES - (CONV_W - 1), CONV_DIM), F32)
    ck = jnp.transpose(cache_k, (0, 1, 3, 4, 2)).reshape(DEPTH, n_pool, FOX_DIM, PAGE_SIZE)
    cv = jnp.transpose(cache_v, (0, 1, 3, 4, 2)).reshape(DEPTH, n_pool, FOX_DIM, PAGE_SIZE)
    lfc = jnp.swapaxes(cache_logf, 2, 3)
    for l in range(DEPTH):
        lp = _layer_params(l, p)

        yp = _ffn_ln(yp, *ffn1, l, tm_p)
        conv0 = jnp.zeros((bp, SUBLANES, CONV_DIM), F32)
        s0 = jnp.zeros((bp, GDN_HEADS, GDN_DK, GDN_DV), F32)
        f = _mixer_front(yp, lp, bp, sp, FOX_TK, conv0, s0, True)
        nkv = sp // FOX_TK
        crow = f["crow"].reshape(bp, nkv, FOX_HEADS, FOX_TK)
        yc = _fox_prompt(f["q"], f["kb"], f["vb"], crow, bp, sp)
        yp = _merge(f["ya"], f["yb"], yc, f["zg"], yp, *mrg, l, tm_p)
        yp = _ffn_ln(yp, *ffn2, l, tm_p)
        to_bshd = lambda a: jnp.transpose(a.reshape(bp, FOX_HEADS, FOX_HD, sp), (0, 3, 1, 2))
        outs[0].append(to_bshd(f["k"]))
        outs[1].append(to_bshd(f["v"]))
        outs[2].append(jnp.transpose(f["lf"], (0, 2, 1)))
        outs[3].append(f["zq"].reshape(bp, sp, CONV_DIM)[:, sp - (CONV_W - 1):])
        outs[4].append(f["s_fin"])

        ys = _ffn_ln(ys, *ffn1, l, tm_s)
        conv0 = jnp.concatenate([jnp.broadcast_to(conv_pad, (db,) + conv_pad.shape), state_conv[l]], axis=1)
        f = _mixer_front(ys, lp, db, ds, tm_s, conv0, state_gdn[l], False)
        crow = jnp.pad(f["crow"][0].reshape(FOX_HEADS, db, ds).transpose(1, 0, 2), ((0, 0), (0, 0), (0, LANES - ds)))
        per_seq = lambda a: a.reshape(db, ds, FOX_DIM)
        yc = _fox_sample(page_table, per_seq(f["q"]), per_seq(f["k"]), per_seq(f["v"]), crow, lfc, ck, cv, l)
        ys = _merge(f["ya"], f["yb"], yc.reshape(ts, FOX_DIM), f["zg"], ys, *mrg, l, tm_s)
        ys = _ffn_ln(ys, *ffn2, l, tm_s)
        outs[5].append(f["k"].reshape(db, ds, FOX_HEADS, FOX_HD))
        outs[6].append(f["v"].reshape(db, ds, FOX_HEADS, FOX_HD))
        outs[7].append(f["lf"][:, SMALL_F:SMALL_F + FOX_HEADS].reshape(db, ds, FOX_HEADS))
        outs[8].append(f["zq"].reshape(db, ds, CONV_DIM)[:, ds - (CONV_W - 1):])
        outs[9].append(f["s_fin"])
        outs[10].append(f["vn"].reshape(db, ds, D_A))

    return (yp.reshape(bp, sp, D_MODEL), ys.reshape(db, ds, D_MODEL)) + tuple(jnp.stack(o) for o in outs)
```

```python
import functools

import jax
import jax.numpy as jnp
from jax import lax
from jax.experimental import pallas as pl
from jax.experimental.pallas import tpu as pltpu

F32 = jnp.float32
BF16 = jnp.bfloat16

D_MODEL = 1024
DEPTH = 2
PAGE_SIZE = 128
D_FF = ((8 * D_MODEL // 3 + 127) // 128) * 128
SGU_CHUNK = 128
SGU_GROUPS = 4
D_A = D_MODEL // 2
GDN_HEADS = 4
GDN_DK = 128
GDN_DV = 128
GDN_CHUNK = 64
CONV_W = 4
CONV_DIM = GDN_HEADS * (2 * GDN_DK + GDN_DV)
FOX_HEADS = 8
FOX_HD = 64
FOX_DIM = FOX_HEADS * FOX_HD
N_BRANCH = 3
ALPHA = (2.0 * DEPTH) ** 0.25
LN_EPS = 1e-5
NORM_EPS = 1e-6

LANES = 128
SUBLANES = 8
VMEM_LIMIT_BYTES = 56 * 1024 * 1024

SEG_A = 2 * D_A
SEG_BQKV = CONV_DIM
SEG_BZ = GDN_HEADS * GDN_DV
SEG_SMALL = LANES
SMALL_F = 0
SMALL_BETA = FOX_HEADS
SMALL_A = FOX_HEADS + GDN_HEADS

HIGHEST = lax.Precision.HIGHEST


def _cparams(*sem):
    return pltpu.CompilerParams(dimension_semantics=sem, vmem_limit_bytes=VMEM_LIMIT_BYTES)


def _const_spec(shape):
    nd = len(shape)
    return pl.BlockSpec(shape, lambda *_: (0,) * nd, pipeline_mode=pl.Buffered(1))


def _layer_spec(shape, layer):
    nd = len(shape) - 1
    return pl.BlockSpec((None,) + tuple(shape[1:]), lambda *_: (layer,) + (0,) * nd, pipeline_mode=pl.Buffered(1))


def _layer_norm(x, g, b):
    mu = jnp.mean(x, axis=-1, keepdims=True)
    xc = x - mu
    var = jnp.mean(xc * xc, axis=-1, keepdims=True)
    return xc * lax.rsqrt(var + LN_EPS) * g + b


def _softplus(x):
    return jnp.maximum(x, 0.0) + jnp.log1p(jnp.exp(-jnp.abs(x)))


def _log_sigmoid(x):
    return jnp.minimum(x, 0.0) - jnp.log1p(jnp.exp(-jnp.abs(x)))


def _dot(a, b):
    return jnp.dot(a, b, preferred_element_type=F32)


def _dot_nt(a, b):
    return lax.dot_general(a, b, (((1,), (1,)), ((), ())), preferred_element_type=F32)


def _bmm(a, b):
    return lax.dot_general(a.astype(BF16), b.astype(BF16), (((2,), (1,)), ((0,), (0,))), preferred_element_type=F32)


def _bmm_nt(a, b):
    return lax.dot_general(a.astype(BF16), b.astype(BF16), (((2,), (2,)), ((0,), (0,))), preferred_element_type=F32)


def _iota(shape, dim):
    return lax.broadcasted_iota(jnp.int32, shape, dim)


FFN_SPLITS = 11


def _ffn_ln_kernel(x_ref, w1_ref, w3_ref, w2_ref, g_ref, b_ref, o_ref):
    x = x_ref[...]
    xb = x.astype(BF16)
    fc = D_FF // FFN_SPLITS
    y = None
    for c in range(FFN_SPLITS):
        h1 = _dot(xb, w1_ref[:, c * fc:(c + 1) * fc])
        h3 = _dot(xb, w3_ref[:, c * fc:(c + 1) * fc])
        a = (h1 * jax.nn.sigmoid(h1) * h3).astype(BF16)
        part = _dot(a, w2_ref[c * fc:(c + 1) * fc, :])
        y = part if y is None else y + part
    o_ref[...] = _layer_norm(ALPHA * x + 0.5 * y, g_ref[...], b_ref[...])


def _ffn_ln(x, w1, w3, w2, g, b, layer, tm):
    t = x.shape[0]
    row = pl.BlockSpec((tm, D_MODEL), lambda i: (i, 0))
    return pl.pallas_call(
        _ffn_ln_kernel,
        grid=(t // tm,),
        in_specs=[row] + [_layer_spec(a.shape, layer) for a in (w1, w3, w2, g, b)],
        out_specs=row,
        out_shape=jax.ShapeDtypeStruct((t, D_MODEL), F32),
        compiler_params=_cparams("parallel"),
        name="ffn_ln",
    )(x, w1, w3, w2, g, b)


_IN_SEGS = (SEG_A, SEG_BQKV, SEG_BZ, FOX_DIM, FOX_DIM, FOX_DIM, N_BRANCH * D_MODEL, SEG_SMALL)
_IN_OFFS = tuple(sum(_IN_SEGS[:i]) for i in range(len(_IN_SEGS)))


PROJ_ROWS = 256


def _spatial_gate(za, ln_g, ln_b, w_ref, bs_ref, seq_rows):
    u = jax.nn.gelu(za[:, :D_A], approximate=True)
    vn = _layer_norm(jax.nn.gelu(za[:, D_A:], approximate=True), ln_g, ln_b)
    r = _iota((SGU_CHUNK, SGU_CHUNK), 0)
    c = _iota((SGU_CHUNK, SGU_CHUNK), 1)
    keep = (r >= c) & ((r // seq_rows) == (c // seq_rows))
    cg = D_A // SGU_GROUPS
    groups = []
    for g in range(SGU_GROUPS):
        wg = jnp.where(keep, w_ref[g], 0.0).astype(BF16)
        bg = bs_ref[g]
        cols = slice(g * cg, (g + 1) * cg)
        groups.append(jnp.concatenate(
            [_dot(wg, vn[n * SGU_CHUNK:(n + 1) * SGU_CHUNK, cols].astype(BF16)) + bg
             for n in range(za.shape[0] // SGU_CHUNK)], axis=0))
    return u * jnp.concatenate(groups, axis=1), vn


def _forget_sums(zs, f_bias, carry_ref, seq_rows):
    lf = _log_sigmoid(zs + f_bias)
    n = zs.shape[0]
    r = _iota((n, n), 0)
    c = _iota((n, n), 1)
    span = min(seq_rows, n)
    tri = jnp.where((r >= c) & ((r // span) == (c // span)), 1.0, 0.0).astype(F32)
    csum = jnp.dot(tri, lf, precision=HIGHEST, preferred_element_type=F32)
    if seq_rows > n:
        @pl.when(pl.program_id(0) % (seq_rows // n) == 0)
        def _():
            carry_ref[...] = jnp.zeros_like(carry_ref)

        csum = csum + carry_ref[0:1, :]
        carry_ref[...] = jnp.broadcast_to(csum[n - 1:n, :], carry_ref.shape)
    return lf, csum


def _in_proj_kernel(x_ref, w_ref, ln_g_ref, ln_b_ref, sw_ref, sb_ref, fb_ref, *refs, kv_transposed, seq_rows, sgu_span):
    xb = x_ref[...].astype(BF16)

    def seg(i):
        return _dot(xb, w_ref[:, _IN_OFFS[i]:_IN_OFFS[i] + _IN_SEGS[i]])

    carry_ref = refs[-1]
    if kv_transposed:
        wkvt_ref, ya_ref, zq_ref, zz_ref, q_ref, kt_ref, vt_ref, ktb_ref, vtb_ref, zg_ref, zs_ref, lf_ref, crow_ref = refs[:-1]
        kt = _dot_nt(wkvt_ref[0:FOX_DIM, :], xb)
        kt_ref[...] = kt
        ktb_ref[...] = kt.astype(BF16)
        vt = _dot_nt(wkvt_ref[FOX_DIM:2 * FOX_DIM, :], xb)
        vt_ref[...] = vt
        vtb_ref[...] = vt.astype(BF16)
    else:
        ya_ref, vn_ref, zq_ref, zz_ref, q_ref, k_ref, v_ref, zg_ref, zs_ref, lf_ref, crow_ref = refs[:-1]
        k_ref[...] = seg(4)
        v_ref[...] = seg(5)
    ya, vn = _spatial_gate(seg(0), ln_g_ref[...], ln_b_ref[...], sw_ref, sb_ref, sgu_span)
    ya_ref[...] = ya
    if not kv_transposed:
        vn_ref[...] = vn
    zq_ref[...] = seg(1)
    zz_ref[...] = seg(2)
    q_ref[...] = (seg(3) * (FOX_HD ** -0.5)).astype(q_ref.dtype)
    zg_ref[...] = seg(6)
    zs = seg(7)
    zs_ref[...] = zs
    lf, csum = _forget_sums(zs, fb_ref[...], carry_ref, seq_rows)
    lf_ref[...] = lf.T[0:FOX_HEADS, :] if kv_transposed else lf
    crow_ref[0] = csum.T[0:FOX_HEADS, :]


def _in_proj(x, w, wkvt, sgu_ln_g, sgu_ln_b, sgu_w, sgu_b, f_bias, n_seq, seq_rows, sgu_span):
    t = x.shape[0]
    tm = PROJ_ROWS
    nb = t // tm
    row = lambda n: pl.BlockSpec((tm, n), lambda i: (i, 0))
    sds = jax.ShapeDtypeStruct
    rows = lambda n, dt=F32: (row(n), sds((t, n), dt))
    consts = (w, sgu_ln_g, sgu_ln_b, sgu_w, sgu_b, f_bias)
    in_specs = [row(D_MODEL)] + [_const_spec(c.shape) for c in consts]
    operands = [x, *consts]
    crow = (pl.BlockSpec((1, FOX_HEADS, tm), lambda i: (i, 0, 0)), sds((nb, FOX_HEADS, tm), F32))
    tail = [rows(N_BRANCH * D_MODEL), rows(SEG_SMALL)]
    if wkvt is None:
        outs = [rows(D_A), rows(D_A), rows(SEG_BQKV), rows(SEG_BZ), rows(FOX_DIM), rows(FOX_DIM), rows(FOX_DIM)] + tail
        outs += [rows(LANES), crow]
    else:
        nt = seq_rows // tm
        f32_kv = (pl.BlockSpec((None, FOX_DIM, tm), lambda i: (i // nt, 0, i % nt)), sds((n_seq, FOX_DIM, seq_rows), F32))
        bf_kv = (pl.BlockSpec((None, None, FOX_DIM, tm), lambda i: (i // nt, i % nt, 0, 0)),
                 sds((n_seq, nt, FOX_DIM, tm), BF16))
        lf = (pl.BlockSpec((None, FOX_HEADS, tm), lambda i: (i // nt, 0, i % nt)), sds((n_seq, FOX_HEADS, seq_rows), F32))
        outs = [rows(D_A), rows(SEG_BQKV), rows(SEG_BZ), rows(FOX_DIM, BF16), f32_kv, f32_kv, bf_kv, bf_kv] + tail
        outs += [lf, crow]
        in_specs.append(_const_spec(wkvt.shape))
        operands.append(wkvt)
    return pl.pallas_call(
        functools.partial(_in_proj_kernel, kv_transposed=wkvt is not None, seq_rows=seq_rows, sgu_span=sgu_span),
        grid=(nb,),
        in_specs=in_specs,
        out_specs=[o[0] for o in outs],
        out_shape=[o[1] for o in outs],
        scratch_shapes=[pltpu.VMEM((SUBLANES, LANES), F32)],
        compiler_params=_cparams("arbitrary"),
        name="in_proj",
    )(*operands)


GDN_PAR_ALOG = 0
GDN_PAR_DTB = 1
GDN_PAR_NORM = 2


def _unit_lower_inverse_minus_identity(a):
    n = GDN_CHUNK
    r = _iota((1, n, n), 1)
    c = _iota((1, n, n), 2)
    same16 = (r // 16) == (c // 16)
    same32 = (r // 32) == (c // 32)
    ad = jnp.where(same16, a, 0.0)
    t = -ad
    p = _bmm(ad, ad)
    t = t + p + _bmm(t, p)
    p = _bmm(p, p)
    t = t + p + _bmm(t, p)
    p = _bmm(p, p)
    t = t + p + _bmm(t, p)
    for off in (jnp.where(same32 & ~same16, a, 0.0), jnp.where(~same32, a, 0.0)):
        m = off + _bmm(t, off)
        t = t - m - _bmm(m, t)
    return t


def _gdn_kernel(zq_ref, zs_ref, zz_ref, conv0_ref, s0_ref, cw_ref, par_ref, y_ref, sfin_ref, xp_scr, s_scr,
                *, ns, nc, rows_in, blocks_per_seq):
    n = GDN_CHUNK
    rows = nc * n
    nch = ns * nc
    blk = pl.program_id(1)

    @pl.when(blk == 0)
    def _():
        xp_scr[:, 0:SUBLANES, :] = conv0_ref[...]
        s_scr[...] = s0_ref[...].reshape(ns * GDN_HEADS, GDN_DK, GDN_DV)

    @pl.when(blk > 0)
    def _():
        xp_scr[:, 0:SUBLANES, :] = xp_scr[:, rows:rows + SUBLANES, :]

    xp_scr[:, SUBLANES:SUBLANES + rows_in, :] = zq_ref[...].reshape(ns, rows_in, CONV_DIM)
    if rows_in < rows:
        xp_scr[:, SUBLANES + rows_in:SUBLANES + rows, :] = jnp.zeros((ns, rows - rows_in, CONV_DIM), F32)

    base = SUBLANES - (CONV_W - 1)
    yc = xp_scr[:, base:base + rows, :] * cw_ref[0:1, :]
    for i in range(1, CONV_W):
        yc = yc + xp_scr[:, base + i:base + i + rows, :] * cw_ref[i:i + 1, :]
    yc = (yc * jax.nn.sigmoid(yc)).reshape(ns * rows, CONV_DIM)

    zs = zs_ref[...].reshape(ns, rows_in, LANES)
    if rows_in < rows:
        zs = jnp.concatenate([zs, jnp.zeros((ns, rows - rows_in, LANES), F32)], axis=1)
    valid = _iota((ns, rows, LANES), 1) < rows_in
    sig = jnp.where(valid, jax.nn.sigmoid(zs), 0.0).reshape(ns * rows, LANES)
    gl = -jnp.exp(par_ref[GDN_PAR_ALOG:GDN_PAR_ALOG + 1, :]) * _softplus(zs + par_ref[GDN_PAR_DTB:GDN_PAR_DTB + 1, :])
    gl = jnp.where(valid, gl, 0.0).reshape(ns * rows, LANES)

    rr = ns * rows
    r = _iota((rr, rr), 0)
    c = _iota((rr, rr), 1)
    tri = jnp.where((r >= c) & ((r // n) == (c // n)), 1.0, 0.0).astype(F32)
    gc = jnp.dot(tri, gl, precision=HIGHEST, preferred_element_type=F32)
    gct = gc.T

    order = [(cc, s, h) for cc in range(nc) for s in range(ns) for h in range(GDN_HEADS)]
    nb = len(order)

    def rows_of(cc, s):
        return slice(s * rows + cc * n, s * rows + (cc + 1) * n)

    def cols3(x, off, width):
        return jnp.stack([x[rows_of(cc, s), off + h * width:off + (h + 1) * width] for cc, s, h in order], axis=0)

    q = cols3(yc, 0, GDN_DK)
    k = cols3(yc, GDN_HEADS * GDN_DK, GDN_DK)
    v = cols3(yc, 2 * GDN_HEADS * GDN_DK, GDN_DV)
    beta = cols3(sig, SMALL_BETA, 1)
    gcol = cols3(gc, SMALL_A, 1)
    grow = jnp.stack([gct[SMALL_A + h:SMALL_A + h + 1, rows_of(cc, s)] for cc, s, h in order], axis=0)
    glast = gcol[:, n - 1:n, :]

    q = q * lax.rsqrt(jnp.sum(q * q, axis=-1, keepdims=True) + NORM_EPS) * (GDN_DK ** -0.5)
    k = k * lax.rsqrt(jnp.sum(k * k, axis=-1, keepdims=True) + NORM_EPS)
    ri = _iota((1, n, n), 1)
    ci = _iota((1, n, n), 2)
    decay = jnp.exp(jnp.where(ri >= ci, gcol - grow, -jnp.inf))
    egc = jnp.exp(gcol)
    qk_kk = _bmm_nt(jnp.concatenate([q, k], axis=1), k)
    qk = qk_kk[:, :n] * decay
    a = jnp.where(ri > ci, beta * qk_kk[:, n:] * decay, 0.0)
    tinv = _unit_lower_inverse_minus_identity(a)
    rhs = jnp.concatenate([v * beta, k * (beta * egc)], axis=-1)
    sol = rhs + _bmm(tinv, rhs)
    w_val = sol[:, :, :GDN_DV]
    k_cum = sol[:, :, GDN_DV:]
    q_dec = q * egc
    k_tail_t = jnp.swapaxes(k * jnp.exp(glast - gcol), 1, 2)
    g_last = jnp.exp(glast)

    norm_g = par_ref[GDN_PAR_NORM:GDN_PAR_NORM + 1, :]
    per = ns * GDN_HEADS
    for cc in range(nc):
        sl = slice(cc * per, (cc + 1) * per)
        s_old = s_scr[...]
        v_new = w_val[sl] - _bmm(k_cum[sl], s_old)
        o = _bmm(q_dec[sl], s_old) + _bmm(qk[sl], v_new)
        s_scr[...] = s_old * g_last[sl] + _bmm(k_tail_t[sl], v_new)
        nv = min(rows_in - cc * n, n)
        for s in range(ns):
            for h in range(GDN_HEADS):
                r0 = s * rows_in + cc * n
                cols = slice(h * GDN_DV, (h + 1) * GDN_DV)
                zz = zz_ref[r0:r0 + nv, cols]
                ov = o[s * GDN_HEADS + h, :nv]
                gate = zz * jax.nn.sigmoid(zz)
                y_ref[r0:r0 + nv, cols] = (ov * lax.rsqrt(jnp.mean(ov * ov, axis=-1, keepdims=True) + NORM_EPS)
                                           * norm_g * gate)

    @pl.when(blk == blocks_per_seq - 1)
    def _():
        sfin_ref[...] = s_scr[...].reshape(ns, GDN_HEADS, GDN_DK, GDN_DV)


def _gdn(zq, zs, zz, conv0, s0, conv_w, par, n_seq, seq_rows, ns, nc, rows_in):
    t = zq.shape[0]
    bps = seq_rows // rows_in
    rows = nc * GDN_CHUNK

    def row(ncol):
        return pl.BlockSpec((ns * rows_in, ncol), lambda b, j: (b * bps + j, 0))

    state = pl.BlockSpec((ns, GDN_HEADS, GDN_DK, GDN_DV), lambda b, j: (b, 0, 0, 0))
    return pl.pallas_call(
        functools.partial(_gdn_kernel, ns=ns, nc=nc, rows_in=rows_in, blocks_per_seq=bps),
        grid=(n_seq // ns, bps),
        in_specs=[row(CONV_DIM), row(LANES), row(SEG_BZ),
                  pl.BlockSpec((ns, SUBLANES, CONV_DIM), lambda b, j: (b, 0, 0)), state,
                  _const_spec(conv_w.shape), _const_spec(par.shape)],
        out_specs=[row(SEG_BZ), state],
        out_shape=[jax.ShapeDtypeStruct((t, SEG_BZ), F32),
                   jax.ShapeDtypeStruct((n_seq, GDN_HEADS, GDN_DK, GDN_DV), F32)],
        scratch_shapes=[pltpu.VMEM((ns, rows + 2 * SUBLANES, CONV_DIM), F32),
                        pltpu.VMEM((ns * GDN_HEADS, GDN_DK, GDN_DV), F32)],
        compiler_params=_cparams("parallel", "arbitrary"),
        name="gdn",
    )(zq, zs, zz, conv0, s0, conv_w, par)


FOX_TQ = 512
FOX_SUB = 128
FOX_TK = PROJ_ROWS


def _fox_prompt_kernel(qb_ref, kt_ref, vt_ref, crow_ref, o_ref, *, nq):
    hp = pl.program_id(1)
    nsub = FOX_TQ // FOX_SUB
    below = _iota((FOX_SUB, FOX_TQ), 0) - _iota((FOX_SUB, FOX_TQ), 1)
    ones = jnp.ones((FOX_HD, FOX_TK), BF16)

    def q_block(qi, _):
        r0 = pl.multiple_of(qi * FOX_TQ, FOX_TQ)
        qs = [[qb_ref[pl.ds(r0 + sb * FOX_SUB, FOX_SUB), e * FOX_HD:(e + 1) * FOX_HD] for sb in range(nsub)]
              for e in range(2)]

        per_q = FOX_TQ // FOX_TK

        def step(j0, carry, diagonal):
            out = list(carry)
            cat = lambda xs: xs[0] if len(xs) == 1 else jnp.concatenate(xs, axis=1)
            heads = lambda ref, u, e: ref[j0 + u, e * FOX_HD:(e + 1) * FOX_HD, :]
            kt = [[heads(kt_ref, u, e) for u in range(per_q)] for e in range(2)]
            vt1 = [[jnp.concatenate([heads(vt_ref, u, e), ones], axis=0) for u in range(per_q)] for e in range(2)]
            nrow = [[-crow_ref[j0 + u, pl.ds(2 * hp + e, 1), :] for u in range(per_q)] for e in range(2)]
            seen = lambda sb: per_q if not diagonal else ((sb + 1) * FOX_SUB - 1) // FOX_TK + 1
            chains = [(e, sb) for e in range(2) for sb in range(nsub)]
            s = {}
            for e, sb in chains:
                n = seen(sb)
                s[e, sb] = _dot(qs[e][sb], cat(kt[e][:n])) + cat(nrow[e][:n])
                if diagonal:
                    s[e, sb] = jnp.where(below[:, :n * FOX_TK] + sb * FOX_SUB >= 0, s[e, sb], -jnp.inf)
            m_new = {c: jnp.maximum(carry[c[0] * nsub + c[1]][0], jnp.max(s[c], axis=-1, keepdims=True)) for c in chains}
            p = {c: jnp.exp(s[c] - m_new[c]).astype(BF16) for c in chains}
            for e, sb in chains:
                m, acc = carry[e * nsub + sb]
                acc = jnp.exp(m - m_new[e, sb]) * acc + _dot_nt(p[e, sb], cat(vt1[e][:seen(sb)]))
                out[e * nsub + sb] = (m_new[e, sb], acc)
            return tuple(out)

        init = tuple((jnp.full((FOX_SUB, 1), -jnp.inf, F32), jnp.zeros((FOX_SUB, 2 * FOX_HD), F32))
                     for _ in range(2 * nsub))
        carry = lax.fori_loop(0, qi, lambda i, cr: step(i * per_q, cr, False), init)
        carry = step(qi * per_q, carry, True)
        for sb in range(nsub):
            acc = [carry[e * nsub + sb][1] for e in range(2)]
            o_ref[pl.ds(r0 + sb * FOX_SUB, FOX_SUB), :] = jnp.concatenate(
                [a[:, :FOX_HD] / a[:, FOX_HD:] for a in acc], axis=-1)
        return 0

    lax.fori_loop(0, nq, q_block, 0)


def _fox_prompt(qb, ktb, vtb, crow, n_seq, seq_rows):
    t = qb.shape[0]
    nkv = seq_rows // FOX_TK
    qspec = pl.BlockSpec((seq_rows, LANES), lambda b, h: (b, h))
    kvspec = pl.BlockSpec((None, nkv, LANES, FOX_TK), lambda b, h: (b, 0, h, 0))
    return pl.pallas_call(
        functools.partial(_fox_prompt_kernel, nq=seq_rows // FOX_TQ),
        grid=(n_seq, FOX_HEADS // 2),
        in_specs=[qspec, kvspec, kvspec,
                  pl.BlockSpec((None, nkv, FOX_HEADS, FOX_TK), lambda b, h: (b, 0, 0, 0))],
        out_specs=qspec,
        out_shape=jax.ShapeDtypeStruct((t, FOX_DIM), F32),
        compiler_params=_cparams("parallel", "parallel"),
        name="fox_prompt",
    )(qb, ktb, vtb, crow)


FS_GROUP = 8
FS_SLOTS = 3
FS_ROWS = FOX_HEADS * SUBLANES


def _fox_sample_kernel(pt_ref, q_ref, kn_ref, vn_ref, crow_ref, lfc_ref, ck_ref, cv_ref, o_ref,
                       kbuf, vbuf, lbuf, sem, *, n_seq, n_pages, layer):
    n_groups = n_pages // FS_GROUP
    total = n_seq * n_groups
    gw = FS_GROUP * PAGE_SIZE

    def copies(f, slot):
        b = f // n_groups
        gi = f % n_groups
        out = []
        for u in range(FS_GROUP):
            page = pt_ref[b, n_pages - 1 - (gi * FS_GROUP + u)]
            lanes = slice(u * PAGE_SIZE, (u + 1) * PAGE_SIZE)
            out.append(pltpu.make_async_copy(ck_ref.at[layer, page], kbuf.at[slot, :, lanes], sem.at[0, slot]))
            out.append(pltpu.make_async_copy(cv_ref.at[layer, page], vbuf.at[slot, :, lanes], sem.at[1, slot]))
            out.append(pltpu.make_async_copy(lfc_ref.at[layer, page],
                                             lbuf.at[slot, u * FOX_HEADS:(u + 1) * FOX_HEADS, :], sem.at[2, slot]))
        return out

    for f0 in range(FS_SLOTS - 1):
        for cp in copies(f0, f0):
            cp.start()

    rr = _iota((FS_ROWS, FOX_DIM), 0)
    cc = _iota((FS_ROWS, FOX_DIM), 1)
    own = (rr // SUBLANES) == (cc // FOX_HD)
    pr = _iota((PAGE_SIZE, PAGE_SIZE), 0)
    pc = _iota((PAGE_SIZE, PAGE_SIZE), 1)
    later = jnp.where(pr > pc, 1.0, 0.0).astype(F32)
    tok = _iota((FS_ROWS, PAGE_SIZE), 0) % SUBLANES
    pos = _iota((FS_ROWS, PAGE_SIZE), 1)
    hl = _iota((SUBLANES, FOX_DIM), 1) // FOX_HD
    pad = jnp.zeros((PAGE_SIZE - SUBLANES, FOX_DIM), F32)

    def rows8(x):
        return jnp.concatenate([jnp.broadcast_to(x[h:h + 1, :], (SUBLANES, x.shape[1])) for h in range(FOX_HEADS)], axis=0)

    def update(carry, s, vbf, v_transposed):
        m, l, acc = carry
        m_new = jnp.maximum(m, jnp.max(s, axis=-1, keepdims=True))
        alpha = jnp.exp(m - m_new)
        p = jnp.exp(s - m_new)
        l = alpha * l + jnp.sum(p, axis=-1, keepdims=True)
        pv = _dot_nt(p.astype(BF16), vbf) if v_transposed else _dot(p.astype(BF16), vbf)
        return m_new, l, alpha * acc + pv

    def sequence(b, _):
        qrep = jnp.concatenate([q_ref[b].astype(BF16)] * FOX_HEADS, axis=0)
        qbd = jnp.where(own, qrep, jnp.zeros_like(qrep))

        def group(gi, carry):
            m, l, acc, tail = carry
            f = b * n_groups + gi
            slot = f % FS_SLOTS
            nxt = f + FS_SLOTS - 1

            @pl.when(nxt < total)
            def _():
                for cp in copies(nxt, nxt % FS_SLOTS):
                    cp.start()

            for cp in copies(f, slot):
                cp.wait()
            lf = lbuf[slot]
            suffix = jnp.dot(lf, later, precision=HIGHEST, preferred_element_type=F32)
            tot = jnp.sum(lf, axis=-1, keepdims=True)
            bias = []
            for u in range(FS_GROUP):
                rows = slice(u * FOX_HEADS, (u + 1) * FOX_HEADS)
                bias.append(rows8(suffix[rows] + tail))
                tail = tail + tot[rows]
            s = _dot(qbd, kbuf[slot].astype(BF16)) + jnp.concatenate(bias, axis=1)
            m, l, acc = update((m, l, acc), s, vbuf[slot].astype(BF16), True)
            return m, l, acc, tail

        init = (jnp.full((FS_ROWS, 1), -jnp.inf, F32), jnp.zeros((FS_ROWS, 1), F32),
                jnp.zeros((FS_ROWS, FOX_DIM), F32), jnp.zeros((FOX_HEADS, 1), F32))
        m, l, acc, _ = lax.fori_loop(0, n_groups, group, init)

        kn = jnp.concatenate([kn_ref[b], pad], axis=0).astype(BF16)
        vn = jnp.concatenate([vn_ref[b], pad], axis=0).astype(BF16)
        s = jnp.where(pos <= tok, _dot_nt(qbd, kn) - rows8(crow_ref[b]), -jnp.inf)
        m, l, acc = update((m, l, acc), s, vn, False)

        o = acc / l
        out = jnp.zeros((SUBLANES, FOX_DIM), F32)
        for h in range(FOX_HEADS):
            out = jnp.where(hl == h, o[h * SUBLANES:(h + 1) * SUBLANES, :], out)
        o_ref[b] = out
        return 0

    lax.fori_loop(0, n_seq, sequence, 0)


def _fox_sample(page_table, q, kn, vn, crow, lfc, ck, cv, layer):
    n_seq, n_pages = page_table.shape
    gw = FS_GROUP * PAGE_SIZE
    whole = lambda a: pl.BlockSpec(a.shape, lambda i, pt: (0,) * a.ndim)
    any_spec = pl.BlockSpec(memory_space=pl.ANY)
    grid_spec = pltpu.PrefetchScalarGridSpec(
        num_scalar_prefetch=1,
        grid=(1,),
        in_specs=[whole(q), whole(kn), whole(vn), whole(crow), any_spec, any_spec, any_spec],
        out_specs=pl.BlockSpec((n_seq, SUBLANES, FOX_DIM), lambda i, pt: (0, 0, 0)),
        scratch_shapes=[pltpu.VMEM((FS_SLOTS, FOX_DIM, gw), F32),
                        pltpu.VMEM((FS_SLOTS, FOX_DIM, gw), F32),
                        pltpu.VMEM((FS_SLOTS, FS_GROUP * FOX_HEADS, PAGE_SIZE), F32),
                        pltpu.SemaphoreType.DMA((3, FS_SLOTS))],
    )
    return pl.pallas_call(
        functools.partial(_fox_sample_kernel, n_seq=n_seq, n_pages=n_pages, layer=layer),
        grid_spec=grid_spec,
        out_shape=jax.ShapeDtypeStruct((n_seq, SUBLANES, FOX_DIM), F32),
        compiler_params=_cparams("arbitrary"),
        name="fox_sample",
    )(page_table, q, kn, vn, crow, lfc, ck, cv)


def _merge_kernel(ya_ref, yb_ref, yc_ref, zg_ref, x_ref, gb_ref, pa_ref, pb_ref, pc_ref, wo_ref, g_ref, b_ref, o_ref):
    merged = None
    for i, (y_ref, p_ref) in enumerate(((ya_ref, pa_ref), (yb_ref, pb_ref), (yc_ref, pc_ref))):
        cols = slice(i * D_MODEL, (i + 1) * D_MODEL)
        gate = jax.nn.sigmoid(zg_ref[:, cols] + gb_ref[:, cols])
        term = gate * _dot(y_ref[...].astype(BF16), p_ref[...])
        merged = term if merged is None else merged + term
    mix = _dot(merged.astype(BF16), wo_ref[...])
    o_ref[...] = _layer_norm(ALPHA * x_ref[...] + mix, g_ref[...], b_ref[...])


def _merge(ya, yb, yc, zg, x, gate_bias, pa, pb, pc, wo, g, b, layer, tm):
    t = x.shape[0]
    row = lambda n: pl.BlockSpec((tm, n), lambda i: (i, 0))
    consts = (gate_bias, pa, pb, pc, wo, g, b)
    return pl.pallas_call(
        _merge_kernel,
        grid=(t // tm,),
        in_specs=[row(D_A), row(SEG_BZ), row(FOX_DIM), row(N_BRANCH * D_MODEL), row(D_MODEL)]
        + [_layer_spec(c.shape, layer) for c in consts],
        out_specs=row(D_MODEL),
        out_shape=jax.ShapeDtypeStruct((t, D_MODEL), F32),
        compiler_params=_cparams("parallel"),
        name="merge",
    )(ya, yb, yc, zg, x, *consts)


_W_IN_K = 2 * D_A + CONV_DIM + 2 * GDN_HEADS + SEG_BZ + FOX_DIM


def _regroup_w_in(w_in):
    o_bqkv = 2 * D_A
    o_beta = o_bqkv + CONV_DIM
    o_a = o_beta + GDN_HEADS
    o_z = o_a + GDN_HEADS
    o_cqkv = o_z + SEG_BZ
    o_f = o_cqkv + 3 * FOX_DIM
    o_gate = o_f + FOX_HEADS
    small = jnp.concatenate([w_in[:, o_f:o_gate], w_in[:, o_beta:o_a], w_in[:, o_a:o_z],
                             jnp.zeros((D_MODEL, SEG_SMALL - FOX_HEADS - 2 * GDN_HEADS), w_in.dtype)], axis=1)
    return jnp.concatenate([w_in[:, :o_beta], w_in[:, o_z:o_cqkv], w_in[:, o_cqkv:o_f], w_in[:, o_gate:], small],
                           axis=1).astype(BF16)


def _lane_row(vals, offset):
    return jnp.zeros((1, LANES), F32).at[0, offset:offset + vals.shape[0]].set(vals.astype(F32))


def _layer_params(l, p):
    par = jnp.zeros((SUBLANES, LANES), F32)
    par = par.at[GDN_PAR_ALOG, SMALL_A:SMALL_A + GDN_HEADS].set(p["gdn_a_log"][l])
    par = par.at[GDN_PAR_DTB, SMALL_A:SMALL_A + GDN_HEADS].set(p["gdn_dt_bias"][l])
    par = par.at[GDN_PAR_NORM, :].set(p["gdn_norm_g"][l])
    conv_w = jnp.concatenate([p["gdn_conv_w"][l], jnp.zeros((SUBLANES - CONV_W, CONV_DIM), F32)], axis=0)
    row = lambda a: a[l].reshape(1, -1)
    return dict(
        w_in=_regroup_w_in(p["w_in"][l]),
        wkvt=p["w_in"][l][:, _W_IN_K:_W_IN_K + 2 * FOX_DIM].T.astype(BF16),
        sgu_ln=(row(p["sgu_ln_g"]), row(p["sgu_ln_b"])),
        sgu_w=p["sgu_w"][l], sgu_b=p["sgu_b"][l],
        f_bias=_lane_row(p["fox_f_bias"][l], SMALL_F),
        conv_w=conv_w, gdn_par=par,
    )


def _mixer_front(x, lp, n_seq, seq_rows, conv0, s0, long_seq):
    if seq_rows >= SGU_CHUNK:
        sgu_w, sgu_b, span = lp["sgu_w"], lp["sgu_b"], SGU_CHUNK
    else:
        rep = SGU_CHUNK // seq_rows
        sgu_w = jnp.tile(lp["sgu_w"][:, :seq_rows, :seq_rows], (1, rep, rep))
        sgu_b = jnp.tile(lp["sgu_b"][:, :seq_rows], (1, rep))
        span = seq_rows
    proj = functools.partial(_in_proj, x, lp["w_in"], sgu_ln_g=lp["sgu_ln"][0], sgu_ln_b=lp["sgu_ln"][1], sgu_w=sgu_w,
                             sgu_b=sgu_b[:, :, None], f_bias=lp["f_bias"], n_seq=n_seq, seq_rows=seq_rows, sgu_span=span)
    if long_seq:
        ya, zq, zz, q, k, v, kb, vb, zg, zs, lf, crow = proj(wkvt=lp["wkvt"])
        vn = None
    else:
        ya, vn, zq, zz, q, k, v, zg, zs, lf, crow = proj(wkvt=None)
        kb = vb = None
    if long_seq:
        ns, nc, rows_in = 1, 4, 4 * GDN_CHUNK
    else:
        ns, nc, rows_in = 8, 1, seq_rows
    yb, s_fin = _gdn(zq, zs, zz, conv0, s0, lp["conv_w"], lp["gdn_par"], n_seq, seq_rows, ns, nc, rows_in)
    return dict(ya=ya, vn=vn, yb=yb, s_fin=s_fin, zq=zq, zg=zg, q=q, k=k, v=v, kb=kb, vb=vb, lf=lf, crow=crow)


def kernel(x_prompt, x_sample, cache_k, cache_v, cache_logf, page_table, state_conv, state_gdn,
           ln1_g, ln1_b, ffn1_w1, ffn1_w3, ffn1_w2, w_in, sgu_ln_g, sgu_ln_b, sgu_w, sgu_b,
           gdn_conv_w, gdn_a_log, gdn_dt_bias, gdn_norm_g, fox_f_bias, gate_bias,
           proj_a, proj_b, proj_c, w_out, ln2_g, ln2_b, ffn2_w1, ffn2_w3, ffn2_w2, ln3_g, ln3_b):
    p = dict(w_in=w_in, sgu_ln_g=sgu_ln_g, sgu_ln_b=sgu_ln_b, sgu_w=sgu_w, sgu_b=sgu_b, gdn_conv_w=gdn_conv_w,
             gdn_a_log=gdn_a_log, gdn_dt_bias=gdn_dt_bias, gdn_norm_g=gdn_norm_g, fox_f_bias=fox_f_bias)
    bp, sp, _ = x_prompt.shape
    db, ds, _ = x_sample.shape
    tp, ts = bp * sp, db * ds
    n_pool = cache_k.shape[1]
    tm_p = 1024 if tp % 1024 == 0 else 256
    tm_s = ts
    assert ts == PROJ_ROWS and sp % FOX_TQ == 0 and ds == SUBLANES and cache_k.shape[2] == PAGE_SIZE
    assert page_table.shape[1] % FS_GROUP == 0 and db % 8 == 0

    bf = lambda a: a.astype(BF16)
    rows3 = lambda a: a[:, None, :]
    ffn1 = (bf(ffn1_w1), bf(ffn1_w3), bf(ffn1_w2), rows3(ln1_g), rows3(ln1_b))
    ffn2 = (bf(ffn2_w1), bf(ffn2_w3), bf(ffn2_w2), rows3(ln3_g), rows3(ln3_b))
    mrg = (rows3(gate_bias), bf(proj_a), bf(proj_b), bf(proj_c), bf(w_out), rows3(ln2_g), rows3(ln2_b))

    yp = x_prompt.reshape(tp, D_MODEL)
    ys = x_sample.reshape(ts, D_MODEL)
    outs = [[] for _ in range(11)]
    conv_pad = jnp.zeros((SUBLANES - (CONV_W - 1), CONV_DIM), F32)
    ck = jnp.transpose(cache_k, (0, 1, 3, 4, 2)).reshape(DEPTH, n_pool, FOX_DIM, PAGE_SIZE)
    cv = jnp.transpose(cache_v, (0, 1, 3, 4, 2)).reshape(DEPTH, n_pool, FOX_DIM, PAGE_SIZE)
    lfc = jnp.swapaxes(cache_logf, 2, 3)
    for l in range(DEPTH):
        lp = _layer_params(l, p)

        yp = _ffn_ln(yp, *ffn1, l, tm_p)
        conv0 = jnp.zeros((bp, SUBLANES, CONV_DIM), F32)
        s0 = jnp.zeros((bp, GDN_HEADS, GDN_DK, GDN_DV), F32)
        f = _mixer_front(yp, lp, bp, sp, conv0, s0, True)
        nkv = sp // FOX_TK
        crow = f["crow"].reshape(bp, nkv, FOX_HEADS, FOX_TK)
        yc = _fox_prompt(f["q"], f["kb"], f["vb"], crow, bp, sp)
        yp = _merge(f["ya"], f["yb"], yc, f["zg"], yp, *mrg, l, min(tm_p, 512))
        yp = _ffn_ln(yp, *ffn2, l, tm_p)
        to_bshd = lambda a: jnp.transpose(a.reshape(bp, FOX_HEADS, FOX_HD, sp), (0, 3, 1, 2))
        outs[0].append(to_bshd(f["k"]))
        outs[1].append(to_bshd(f["v"]))
        outs[2].append(jnp.transpose(f["lf"], (0, 2, 1)))
        outs[3].append(f["zq"].reshape(bp, sp, CONV_DIM)[:, sp - (CONV_W - 1):])
        outs[4].append(f["s_fin"])

        ys = _ffn_ln(ys, *ffn1, l, tm_s)
        conv0 = jnp.concatenate([jnp.broadcast_to(conv_pad, (db,) + conv_pad.shape), state_conv[l]], axis=1)
        f = _mixer_front(ys, lp, db, ds, conv0, state_gdn[l], False)
        crow = jnp.pad(f["crow"][0].reshape(FOX_HEADS, db, ds).transpose(1, 0, 2), ((0, 0), (0, 0), (0, LANES - ds)))
        per_seq = lambda a: a.reshape(db, ds, FOX_DIM)
        yc = _fox_sample(page_table, per_seq(f["q"]), per_seq(f["k"]), per_seq(f["v"]), crow, lfc, ck, cv, l)
        ys = _merge(f["ya"], f["yb"], yc.reshape(ts, FOX_DIM), f["zg"], ys, *mrg, l, tm_s)
        ys = _ffn_ln(ys, *ffn2, l, tm_s)
        outs[5].append(f["k"].reshape(db, ds, FOX_HEADS, FOX_HD))
        outs[6].append(f["v"].reshape(db, ds, FOX_HEADS, FOX_HD))
        outs[7].append(f["lf"][:, SMALL_F:SMALL_F + FOX_HEADS].reshape(db, ds, FOX_HEADS))
        outs[8].append(f["zq"].reshape(db, ds, CONV_DIM)[:, ds - (CONV_W - 1):])
        outs[9].append(f["s_fin"])
        outs[10].append(f["vn"].reshape(db, ds, D_A))

    return (yp.reshape(bp, sp, D_MODEL), ys.reshape(db, ds, D_MODEL)) + tuple(jnp.stack(o) for o in outs)
```

```python
import functools

import jax
import jax.numpy as jnp
from jax import lax
from jax.experimental import pallas as pl
from jax.experimental.pallas import tpu as pltpu

F32 = jnp.float32
BF16 = jnp.bfloat16

D_MODEL = 1024
DEPTH = 2
PAGE_SIZE = 128
D_FF = ((8 * D_MODEL // 3 + 127) // 128) * 128
SGU_CHUNK = 128
SGU_GROUPS = 4
D_A = D_MODEL // 2
GDN_HEADS = 4
GDN_DK = 128
GDN_DV = 128
GDN_CHUNK = 64
CONV_W = 4
CONV_DIM = GDN_HEADS * (2 * GDN_DK + GDN_DV)
FOX_HEADS = 8
FOX_HD = 64
FOX_DIM = FOX_HEADS * FOX_HD
N_BRANCH = 3
ALPHA = (2.0 * DEPTH) ** 0.25
LN_EPS = 1e-5
NORM_EPS = 1e-6

LANES = 128
SUBLANES = 8
VMEM_LIMIT_BYTES = 56 * 1024 * 1024

SEG_A = 2 * D_A
SEG_BQKV = CONV_DIM
SEG_BZ = GDN_HEADS * GDN_DV
SEG_SMALL = LANES
SMALL_F = 0
SMALL_BETA = FOX_HEADS
SMALL_A = FOX_HEADS + GDN_HEADS

HIGHEST = lax.Precision.HIGHEST


def _cparams(*sem):
    return pltpu.CompilerParams(dimension_semantics=sem, vmem_limit_bytes=VMEM_LIMIT_BYTES)


def _const_spec(shape):
    nd = len(shape)
    return pl.BlockSpec(shape, lambda *_: (0,) * nd, pipeline_mode=pl.Buffered(1))


def _layer_spec(shape, layer):
    nd = len(shape) - 1
    return pl.BlockSpec((None,) + tuple(shape[1:]), lambda *_: (layer,) + (0,) * nd, pipeline_mode=pl.Buffered(1))


def _layer_norm(x, g, b):
    mu = jnp.mean(x, axis=-1, keepdims=True)
    xc = x - mu
    var = jnp.mean(xc * xc, axis=-1, keepdims=True)
    return xc * lax.rsqrt(var + LN_EPS) * g + b


def _softplus(x):
    return jnp.maximum(x, 0.0) + jnp.log1p(jnp.exp(-jnp.abs(x)))


def _log_sigmoid(x):
    return jnp.minimum(x, 0.0) - jnp.log1p(jnp.exp(-jnp.abs(x)))


def _dot(a, b):
    return jnp.dot(a, b, preferred_element_type=F32)


def _dot_nt(a, b):
    return lax.dot_general(a, b, (((1,), (1,)), ((), ())), preferred_element_type=F32)


def _bmm(a, b):
    return lax.dot_general(a.astype(BF16), b.astype(BF16), (((2,), (1,)), ((0,), (0,))), preferred_element_type=F32)


def _bmm_nt(a, b):
    return lax.dot_general(a.astype(BF16), b.astype(BF16), (((2,), (2,)), ((0,), (0,))), preferred_element_type=F32)


def _iota(shape, dim):
    return lax.broadcasted_iota(jnp.int32, shape, dim)


FFN_SPLITS = 11


def _ffn_ln_kernel(x_ref, w1_ref, w3_ref, w2_ref, g_ref, b_ref, o_ref):
    x = x_ref[...]
    xb = x.astype(BF16)
    fc = D_FF // FFN_SPLITS
    y = None
    for c in range(FFN_SPLITS):
        h1 = _dot(xb, w1_ref[:, c * fc:(c + 1) * fc])
        h3 = _dot(xb, w3_ref[:, c * fc:(c + 1) * fc])
        a = (h1 * jax.nn.sigmoid(h1) * h3).astype(BF16)
        part = _dot(a, w2_ref[c * fc:(c + 1) * fc, :])
        y = part if y is None else y + part
    o_ref[...] = _layer_norm(ALPHA * x + 0.5 * y, g_ref[...], b_ref[...])


def _ffn_ln(x, w1, w3, w2, g, b, layer, tm):
    t = x.shape[0]
    row = pl.BlockSpec((tm, D_MODEL), lambda i: (i, 0))
    return pl.pallas_call(
        _ffn_ln_kernel,
        grid=(t // tm,),
        in_specs=[row] + [_layer_spec(a.shape, layer) for a in (w1, w3, w2, g, b)],
        out_specs=row,
        out_shape=jax.ShapeDtypeStruct((t, D_MODEL), F32),
        compiler_params=_cparams("parallel"),
        name="ffn_ln",
    )(x, w1, w3, w2, g, b)


_IN_SEGS = (SEG_A, SEG_BQKV, SEG_BZ, FOX_DIM, FOX_DIM, FOX_DIM, N_BRANCH * D_MODEL, SEG_SMALL)
_IN_OFFS = tuple(sum(_IN_SEGS[:i]) for i in range(len(_IN_SEGS)))


PROJ_ROWS = 256


def _spatial_gate(za, ln_g, ln_b, w_ref, bs_ref, seq_rows):
    u = jax.nn.gelu(za[:, :D_A], approximate=True)
    vn = _layer_norm(jax.nn.gelu(za[:, D_A:], approximate=True), ln_g, ln_b)
    r = _iota((SGU_CHUNK, SGU_CHUNK), 0)
    c = _iota((SGU_CHUNK, SGU_CHUNK), 1)
    keep = (r >= c) & ((r // seq_rows) == (c // seq_rows))
    cg = D_A // SGU_GROUPS
    groups = []
    for g in range(SGU_GROUPS):
        wg = jnp.where(keep, w_ref[g], 0.0).astype(BF16)
        bg = bs_ref[g]
        cols = slice(g * cg, (g + 1) * cg)
        groups.append(jnp.concatenate(
            [_dot(wg, vn[n * SGU_CHUNK:(n + 1) * SGU_CHUNK, cols].astype(BF16)) + bg
             for n in range(za.shape[0] // SGU_CHUNK)], axis=0))
    return u * jnp.concatenate(groups, axis=1), vn


def _forget_sums(zs, f_bias, carry_ref, seq_rows):
    lf = _log_sigmoid(zs + f_bias)
    n = zs.shape[0]
    r = _iota((n, n), 0)
    c = _iota((n, n), 1)
    span = min(seq_rows, n)
    tri = jnp.where((r >= c) & ((r // span) == (c // span)), 1.0, 0.0).astype(F32)
    csum = jnp.dot(tri, lf, precision=HIGHEST, preferred_element_type=F32)
    if seq_rows > n:
        @pl.when(pl.program_id(0) % (seq_rows // n) == 0)
        def _():
            carry_ref[...] = jnp.zeros_like(carry_ref)

        csum = csum + carry_ref[0:1, :]
        carry_ref[...] = jnp.broadcast_to(csum[n - 1:n, :], carry_ref.shape)
    return lf, csum


def _in_proj_kernel(x_ref, w_ref, ln_g_ref, ln_b_ref, sw_ref, sb_ref, fb_ref, *refs, kv_transposed, n_stacked, seq_rows,
                    sgu_span):
    xb = x_ref[...].astype(BF16)

    def seg(i):
        return _dot(xb, w_ref[:, _IN_OFFS[i]:_IN_OFFS[i] + _IN_SEGS[i]])

    carry_ref = refs[-1]
    if kv_transposed:
        wkvt_ref = refs[0]
        ya_ref, zq_ref, zz_ref, q_ref, kt_ref, vt_ref, ktb_ref, vtb_ref, zg_ref, zs_ref, lf_ref, crow_ref = refs[1 + n_stacked:-1]
        kt = _dot_nt(wkvt_ref[0:FOX_DIM, :], xb)
        vt = _dot_nt(wkvt_ref[FOX_DIM:2 * FOX_DIM, :], xb)
        ktb_ref[...] = kt.astype(BF16)
        vtb_ref[...] = vt.astype(BF16)
        if n_stacked:
            kt_ref[...] = kt
            vt_ref[...] = vt
        else:
            for ref, val in ((kt_ref, kt), (vt_ref, vt)):
                ref[0] = val
                for other in range(1, ref.shape[0]):
                    ref[other] = jnp.zeros_like(val)
    else:
        ya_ref, vn_ref, zq_ref, zz_ref, q_ref, k_ref, v_ref, zg_ref, zs_ref, lf_ref, crow_ref = refs[:-1]
        k_ref[...] = seg(4)
        v_ref[...] = seg(5)
    ya, vn = _spatial_gate(seg(0), ln_g_ref[...], ln_b_ref[...], sw_ref, sb_ref, sgu_span)
    ya_ref[...] = ya.astype(ya_ref.dtype)
    if not kv_transposed:
        vn_ref[...] = vn
    zq_ref[...] = seg(1)
    zz_ref[...] = seg(2)
    q_ref[...] = (seg(3) * (FOX_HD ** -0.5)).astype(q_ref.dtype)
    zg_ref[...] = seg(6)
    zs = seg(7)
    zs_ref[...] = zs
    lf, csum = _forget_sums(zs, fb_ref[...], carry_ref, seq_rows)
    lf_ref[...] = lf.T[0:FOX_HEADS, :] if kv_transposed else lf
    crow_ref[0] = csum.T[0:FOX_HEADS, :]


def _in_proj(x, w, wkvt, sgu_ln_g, sgu_ln_b, sgu_w, sgu_b, f_bias, n_seq, seq_rows, sgu_span, layer=0, kv_stack=None):
    t = x.shape[0]
    tm = PROJ_ROWS
    nb = t // tm
    row = lambda n: pl.BlockSpec((tm, n), lambda i: (i, 0))
    sds = jax.ShapeDtypeStruct
    rows = lambda n, dt=F32: (row(n), sds((t, n), dt))
    consts = (w, sgu_ln_g, sgu_ln_b, sgu_w, sgu_b, f_bias)
    in_specs = [row(D_MODEL)] + [_const_spec(c.shape) for c in consts]
    operands = [x, *consts]
    crow = (pl.BlockSpec((1, FOX_HEADS, tm), lambda i: (i, 0, 0)), sds((nb, FOX_HEADS, tm), F32))
    tail = [rows(N_BRANCH * D_MODEL), rows(SEG_SMALL)]
    aliases = {}
    if wkvt is None:
        outs = [rows(D_A), rows(D_A), rows(SEG_BQKV), rows(SEG_BZ), rows(FOX_DIM), rows(FOX_DIM), rows(FOX_DIM)] + tail
        outs += [rows(LANES), crow]
    else:
        nt = seq_rows // tm
        if kv_stack is None:
            assert layer == 0
            kv_block = pl.BlockSpec((DEPTH, None, FOX_DIM, tm), lambda i: (0, i // nt, 0, i % nt))
        else:
            kv_block = pl.BlockSpec((None, None, FOX_DIM, tm), lambda i: (layer, i // nt, 0, i % nt))
        f32_kv = (kv_block, sds((DEPTH, n_seq, FOX_DIM, seq_rows), F32))
        bf_kv = (pl.BlockSpec((None, None, FOX_DIM, tm), lambda i: (i // nt, i % nt, 0, 0)),
                 sds((n_seq, nt, FOX_DIM, tm), BF16))
        lf = (pl.BlockSpec((None, FOX_HEADS, tm), lambda i: (i // nt, 0, i % nt)), sds((n_seq, FOX_HEADS, seq_rows), F32))
        outs = [rows(D_A, BF16), rows(SEG_BQKV), rows(SEG_BZ), rows(FOX_DIM, BF16), f32_kv, f32_kv, bf_kv, bf_kv] + tail
        outs += [lf, crow]
        in_specs.append(_const_spec(wkvt.shape))
        operands.append(wkvt)
        if kv_stack is not None:
            aliases = {len(operands): 4, len(operands) + 1: 5}
            in_specs += [pl.BlockSpec(memory_space=pl.ANY)] * 2
            operands += list(kv_stack)
    return pl.pallas_call(
        functools.partial(_in_proj_kernel, kv_transposed=wkvt is not None, n_stacked=len(aliases), seq_rows=seq_rows,
                          sgu_span=sgu_span),
        grid=(nb,),
        in_specs=in_specs,
        out_specs=[o[0] for o in outs],
        out_shape=[o[1] for o in outs],
        scratch_shapes=[pltpu.VMEM((SUBLANES, LANES), F32)],
        input_output_aliases=aliases,
        compiler_params=_cparams("arbitrary"),
        name="in_proj",
    )(*operands)


GDN_PAR_ALOG = 0
GDN_PAR_DTB = 1
GDN_PAR_NORM = 2


def _unit_lower_inverse_minus_identity(a):
    n = GDN_CHUNK
    r = _iota((1, n, n), 1)
    c = _iota((1, n, n), 2)
    same16 = (r // 16) == (c // 16)
    same32 = (r // 32) == (c // 32)
    ad = jnp.where(same16, a, 0.0)
    adb = ad.astype(BF16)
    t = -ad
    p = _bmm(adb, adb)
    for step in range(3):
        pb = p.astype(BF16)
        t = t + p + _bmm(-adb if step == 0 else t, pb)
        if step < 2:
            p = _bmm(pb, pb)
    for off in (jnp.where(same32 & ~same16, a, 0.0), jnp.where(~same32, a, 0.0)):
        tb = t.astype(BF16)
        m = off + _bmm(tb, off)
        t = t - m - _bmm(m, tb)
    return t


def _gdn_kernel(zq_ref, zs_ref, zz_ref, conv0_ref, s0_ref, cw_ref, par_ref, y_ref, sfin_ref, xp_scr, s_scr,
                *, ns, nc, rows_in, blocks_per_seq):
    n = GDN_CHUNK
    rows = nc * n
    nch = ns * nc
    blk = pl.program_id(1)

    @pl.when(blk == 0)
    def _():
        xp_scr[:, 0:SUBLANES, :] = conv0_ref[...]
        s_scr[...] = s0_ref[...].reshape(ns * GDN_HEADS, GDN_DK, GDN_DV)

    @pl.when(blk > 0)
    def _():
        xp_scr[:, 0:SUBLANES, :] = xp_scr[:, rows:rows + SUBLANES, :]

    xp_scr[:, SUBLANES:SUBLANES + rows_in, :] = zq_ref[...].reshape(ns, rows_in, CONV_DIM)
    if rows_in < rows:
        xp_scr[:, SUBLANES + rows_in:SUBLANES + rows, :] = jnp.zeros((ns, rows - rows_in, CONV_DIM), F32)

    base = SUBLANES - (CONV_W - 1)
    yc = xp_scr[:, base:base + rows, :] * cw_ref[0:1, :]
    for i in range(1, CONV_W):
        yc = yc + xp_scr[:, base + i:base + i + rows, :] * cw_ref[i:i + 1, :]
    yc = (yc * jax.nn.sigmoid(yc)).reshape(ns * rows, CONV_DIM)

    zs = zs_ref[...].reshape(ns, rows_in, LANES)
    if rows_in < rows:
        zs = jnp.concatenate([zs, jnp.zeros((ns, rows - rows_in, LANES), F32)], axis=1)
    valid = _iota((ns, rows, LANES), 1) < rows_in
    sig = jnp.where(valid, jax.nn.sigmoid(zs), 0.0).reshape(ns * rows, LANES)
    gl = -jnp.exp(par_ref[GDN_PAR_ALOG:GDN_PAR_ALOG + 1, :]) * _softplus(zs + par_ref[GDN_PAR_DTB:GDN_PAR_DTB + 1, :])
    gl = jnp.where(valid, gl, 0.0).reshape(ns * rows, LANES)

    rr = ns * rows
    r = _iota((rr, rr), 0)
    c = _iota((rr, rr), 1)
    tri = jnp.where((r >= c) & ((r // n) == (c // n)), 1.0, 0.0).astype(F32)
    gc = jnp.dot(tri, gl, precision=HIGHEST, preferred_element_type=F32)
    gct = gc.T

    order = [(cc, s, h) for cc in range(nc) for s in range(ns) for h in range(GDN_HEADS)]
    nb = len(order)

    def rows_of(cc, s):
        return slice(s * rows + cc * n, s * rows + (cc + 1) * n)

    def cols3(x, off, width):
        return jnp.stack([x[rows_of(cc, s), off + h * width:off + (h + 1) * width] for cc, s, h in order], axis=0)

    def l2_heads(off, scale):
        heads = []
        for h in range(GDN_HEADS):
            xh = yc[:, off + h * GDN_DK:off + (h + 1) * GDN_DK]
            heads.append(xh * (lax.rsqrt(jnp.sum(xh * xh, axis=-1, keepdims=True) + NORM_EPS) * scale))
        return jnp.concatenate(heads, axis=1)

    q = cols3(l2_heads(0, GDN_DK ** -0.5), 0, GDN_DK)
    k = cols3(l2_heads(GDN_HEADS * GDN_DK, 1.0), 0, GDN_DK)
    v = cols3(yc, 2 * GDN_HEADS * GDN_DK, GDN_DV)
    beta = cols3(sig, SMALL_BETA, 1)
    gcol = cols3(gc, SMALL_A, 1)
    grow = jnp.stack([gct[SMALL_A + h:SMALL_A + h + 1, rows_of(cc, s)] for cc, s, h in order], axis=0)
    glast = gcol[:, n - 1:n, :]

    ri = _iota((1, n, n), 1)
    ci = _iota((1, n, n), 2)
    decay = jnp.exp(jnp.where(ri >= ci, gcol - grow, -jnp.inf))
    egc = jnp.exp(gcol)
    qk_kk = _bmm_nt(jnp.concatenate([q, k], axis=1), k)
    qk = qk_kk[:, :n] * decay
    a = jnp.where(ri > ci, beta * qk_kk[:, n:] * decay, 0.0)
    tinv = _unit_lower_inverse_minus_identity(a)
    rhs = jnp.concatenate([v * beta, k * (beta * egc)], axis=-1)
    sol = rhs + _bmm(tinv, rhs)
    w_val = sol[:, :, :GDN_DV]
    k_cum = sol[:, :, GDN_DV:]
    q_dec = q * egc
    k_tail_t = jnp.swapaxes(k * jnp.exp(glast - gcol), 1, 2)
    g_last = jnp.exp(glast)

    norm_g = par_ref[GDN_PAR_NORM:GDN_PAR_NORM + 1, :]
    per = ns * GDN_HEADS
    for cc in range(nc):
        sl = slice(cc * per, (cc + 1) * per)
        s_old = s_scr[...]
        v_new = w_val[sl] - _bmm(k_cum[sl], s_old)
        o = _bmm(q_dec[sl], s_old) + _bmm(qk[sl], v_new)
        s_scr[...] = s_old * g_last[sl] + _bmm(k_tail_t[sl], v_new)
        nv = min(rows_in - cc * n, n)
        for s in range(ns):
            for h in range(GDN_HEADS):
                r0 = s * rows_in + cc * n
                cols = slice(h * GDN_DV, (h + 1) * GDN_DV)
                zz = zz_ref[r0:r0 + nv, cols]
                ov = o[s * GDN_HEADS + h, :nv]
                gate = zz * jax.nn.sigmoid(zz)
                y_ref[r0:r0 + nv, cols] = (ov * lax.rsqrt(jnp.mean(ov * ov, axis=-1, keepdims=True) + NORM_EPS)
                                           * norm_g * gate).astype(y_ref.dtype)

    @pl.when(blk == blocks_per_seq - 1)
    def _():
        sfin_ref[...] = s_scr[...].reshape(ns, GDN_HEADS, GDN_DK, GDN_DV)


def _gdn(zq, zs, zz, conv0, s0, conv_w, par, n_seq, seq_rows, ns, nc, rows_in, y_dtype):
    t = zq.shape[0]
    bps = seq_rows // rows_in
    rows = nc * GDN_CHUNK

    def row(ncol):
        return pl.BlockSpec((ns * rows_in, ncol), lambda b, j: (b * bps + j, 0))

    state = pl.BlockSpec((ns, GDN_HEADS, GDN_DK, GDN_DV), lambda b, j: (b, 0, 0, 0))
    return pl.pallas_call(
        functools.partial(_gdn_kernel, ns=ns, nc=nc, rows_in=rows_in, blocks_per_seq=bps),
        grid=(n_seq // ns, bps),
        in_specs=[row(CONV_DIM), row(LANES), row(SEG_BZ),
                  pl.BlockSpec((ns, SUBLANES, CONV_DIM), lambda b, j: (b, 0, 0)), state,
                  _const_spec(conv_w.shape), _const_spec(par.shape)],
        out_specs=[row(SEG_BZ), state],
        out_shape=[jax.ShapeDtypeStruct((t, SEG_BZ), y_dtype),
                   jax.ShapeDtypeStruct((n_seq, GDN_HEADS, GDN_DK, GDN_DV), F32)],
        scratch_shapes=[pltpu.VMEM((ns, rows + 2 * SUBLANES, CONV_DIM), F32),
                        pltpu.VMEM((ns * GDN_HEADS, GDN_DK, GDN_DV), F32)],
        compiler_params=_cparams("parallel", "arbitrary"),
        name="gdn",
    )(zq, zs, zz, conv0, s0, conv_w, par)


FOX_TQ = 1024
FOX_SUB = 128
FOX_TK = PROJ_ROWS


def _fox_prompt_kernel(qb_ref, kt_ref, vt_ref, crow_ref, o_ref, *, nq):
    hp = pl.program_id(1)
    nsub = FOX_TQ // FOX_SUB
    below = _iota((FOX_SUB, FOX_TQ), 0) - _iota((FOX_SUB, FOX_TQ), 1)
    ones = jnp.ones((FOX_HD, FOX_TK), BF16)

    def q_block(qi, _):
        r0 = pl.multiple_of(qi * FOX_TQ, FOX_TQ)
        qs = [[qb_ref[pl.ds(r0 + sb * FOX_SUB, FOX_SUB), e * FOX_HD:(e + 1) * FOX_HD] for sb in range(nsub)]
              for e in range(2)]

        per_q = FOX_TQ // FOX_TK

        def step(j0, carry, diagonal):
            out = list(carry)
            cat = lambda xs: xs[0] if len(xs) == 1 else jnp.concatenate(xs, axis=1)
            heads = lambda ref, u, e: ref[j0 + u, e * FOX_HD:(e + 1) * FOX_HD, :]
            kt = [[heads(kt_ref, u, e) for u in range(per_q)] for e in range(2)]
            vt1 = [[jnp.concatenate([heads(vt_ref, u, e), ones], axis=0) for u in range(per_q)] for e in range(2)]
            nrow = [[-crow_ref[j0 + u, pl.ds(2 * hp + e, 1), :] for u in range(per_q)] for e in range(2)]
            seen = lambda sb: per_q if not diagonal else ((sb + 1) * FOX_SUB - 1) // FOX_TK + 1
            chains = [(e, sb) for e in range(2) for sb in range(nsub)]
            s = {}
            for e, sb in chains:
                n = seen(sb)
                s[e, sb] = _dot(qs[e][sb], cat(kt[e][:n])) + cat(nrow[e][:n])
                if diagonal:
                    s[e, sb] = jnp.where(below[:, :n * FOX_TK] + sb * FOX_SUB >= 0, s[e, sb], -jnp.inf)
            m_new = {c: jnp.maximum(carry[c[0] * nsub + c[1]][0], jnp.max(s[c], axis=-1, keepdims=True)) for c in chains}
            p = {c: jnp.exp(s[c] - m_new[c]).astype(BF16) for c in chains}
            for e, sb in chains:
                m, acc = carry[e * nsub + sb]
                acc = jnp.exp(m - m_new[e, sb]) * acc + _dot_nt(p[e, sb], cat(vt1[e][:seen(sb)]))
                out[e * nsub + sb] = (m_new[e, sb], acc)
            return tuple(out)

        init = tuple((jnp.full((FOX_SUB, 1), -jnp.inf, F32), jnp.zeros((FOX_SUB, 2 * FOX_HD), F32))
                     for _ in range(2 * nsub))
        carry = lax.fori_loop(0, qi, lambda i, cr: step(i * per_q, cr, False), init)
        carry = step(qi * per_q, carry, True)
        for sb in range(nsub):
            acc = [carry[e * nsub + sb][1] for e in range(2)]
            o_ref[pl.ds(r0 + sb * FOX_SUB, FOX_SUB), :] = jnp.concatenate(
                [a[:, :FOX_HD] / a[:, FOX_HD:] for a in acc], axis=-1).astype(o_ref.dtype)
        return 0

    lax.fori_loop(0, nq, q_block, 0)


def _fox_prompt(qb, ktb, vtb, crow, n_seq, seq_rows):
    t = qb.shape[0]
    nkv = seq_rows // FOX_TK
    qspec = pl.BlockSpec((seq_rows, LANES), lambda b, h: (b, h))
    kvspec = pl.BlockSpec((None, nkv, LANES, FOX_TK), lambda b, h: (b, 0, h, 0))
    return pl.pallas_call(
        functools.partial(_fox_prompt_kernel, nq=seq_rows // FOX_TQ),
        grid=(n_seq, FOX_HEADS // 2),
        in_specs=[qspec, kvspec, kvspec,
                  pl.BlockSpec((None, nkv, FOX_HEADS, FOX_TK), lambda b, h: (b, 0, 0, 0))],
        out_specs=qspec,
        out_shape=jax.ShapeDtypeStruct((t, FOX_DIM), BF16),
        compiler_params=_cparams("parallel", "parallel"),
        name="fox_prompt",
    )(qb, ktb, vtb, crow)


FS_GROUP = 8
FS_SLOTS = 3
FS_ROWS = FOX_HEADS * SUBLANES


class _PagedAttention:
    def __init__(self, pt_ref, lfc_ref, ck_ref, cv_ref, kbuf, vbuf, lbuf, sem, n_seq, n_pages, layer):
        self.pt_ref, self.lfc_ref, self.ck_ref, self.cv_ref = pt_ref, lfc_ref, ck_ref, cv_ref
        self.kbuf, self.vbuf, self.lbuf, self.sem = kbuf, vbuf, lbuf, sem
        self.n_pages, self.layer = n_pages, layer
        self.n_groups = n_pages // FS_GROUP
        self.total = n_seq * self.n_groups
        rr = _iota((FS_ROWS, FOX_DIM), 0)
        cc = _iota((FS_ROWS, FOX_DIM), 1)
        self.own = (rr // SUBLANES) == (cc // FOX_HD)
        pr = _iota((PAGE_SIZE, PAGE_SIZE), 0)
        pc = _iota((PAGE_SIZE, PAGE_SIZE), 1)
        self.later = jnp.where(pr > pc, 1.0, 0.0).astype(F32)

    def copies(self, f, slot):
        b = f // self.n_groups
        gi = f % self.n_groups
        out = []
        for u in range(FS_GROUP):
            page = self.pt_ref[b, self.n_pages - 1 - (gi * FS_GROUP + u)]
            lanes = slice(u * PAGE_SIZE, (u + 1) * PAGE_SIZE)
            heads = slice(u * FOX_HEADS, (u + 1) * FOX_HEADS)
            out.append(pltpu.make_async_copy(self.ck_ref.at[self.layer, page], self.kbuf.at[slot, :, lanes],
                                             self.sem.at[0, slot]))
            out.append(pltpu.make_async_copy(self.cv_ref.at[self.layer, page], self.vbuf.at[slot, :, lanes],
                                             self.sem.at[1, slot]))
            out.append(pltpu.make_async_copy(self.lfc_ref.at[self.layer, page], self.lbuf.at[slot, heads, :],
                                             self.sem.at[2, slot]))
        return out

    def prime(self):
        for f0 in range(FS_SLOTS - 1):
            for cp in self.copies(f0, f0):
                cp.start()

    @staticmethod
    def rows8(x):
        return jnp.concatenate([jnp.broadcast_to(x[h:h + 1, :], (SUBLANES, x.shape[1])) for h in range(FOX_HEADS)], axis=0)

    @staticmethod
    def update(carry, s, vbf, v_transposed):
        m, l, acc = carry
        m_new = jnp.maximum(m, jnp.max(s, axis=-1, keepdims=True))
        alpha = jnp.exp(m - m_new)
        p = jnp.exp(s - m_new)
        l = alpha * l + jnp.sum(p, axis=-1, keepdims=True)
        pv = _dot_nt(p.astype(BF16), vbf) if v_transposed else _dot(p.astype(BF16), vbf)
        return m_new, l, alpha * acc + pv

    def begin(self, q):
        qrep = jnp.concatenate([q.astype(BF16)] * FOX_HEADS, axis=0)
        qbd = jnp.where(self.own, qrep, jnp.zeros_like(qrep))
        state = (jnp.full((FS_ROWS, 1), -jnp.inf, F32), jnp.zeros((FS_ROWS, 1), F32),
                 jnp.zeros((FS_ROWS, FOX_DIM), F32), jnp.zeros((FOX_HEADS, 1), F32))
        return qbd, state

    def group(self, f, qbd, state):
        m, l, acc, tail = state
        slot = f % FS_SLOTS
        nxt = f + FS_SLOTS - 1

        @pl.when(nxt < self.total)
        def _():
            for cp in self.copies(nxt, nxt % FS_SLOTS):
                cp.start()

        for cp in self.copies(f, slot):
            cp.wait()
        lf = self.lbuf[slot]
        suffix = jnp.dot(lf, self.later, precision=HIGHEST, preferred_element_type=F32)
        tot = jnp.sum(lf, axis=-1, keepdims=True)
        bias = []
        for u in range(FS_GROUP):
            rows = slice(u * FOX_HEADS, (u + 1) * FOX_HEADS)
            bias.append(self.rows8(suffix[rows] + tail))
            tail = tail + tot[rows]
        s = _dot(qbd, self.kbuf[slot].astype(BF16)) + jnp.concatenate(bias, axis=1)
        m, l, acc = self.update((m, l, acc), s, self.vbuf[slot].astype(BF16), True)
        return m, l, acc, tail

    def finish(self, qbd, state, kn, vn, crow):
        m, l, acc, _ = state
        pad = jnp.zeros((PAGE_SIZE - SUBLANES, FOX_DIM), F32)
        knp = jnp.concatenate([kn, pad], axis=0).astype(BF16)
        vnp = jnp.concatenate([vn, pad], axis=0).astype(BF16)
        tok = _iota((FS_ROWS, PAGE_SIZE), 0) % SUBLANES
        pos = _iota((FS_ROWS, PAGE_SIZE), 1)
        s = jnp.where(pos <= tok, _dot_nt(qbd, knp) - self.rows8(crow), -jnp.inf)
        m, l, acc = self.update((m, l, acc), s, vnp, False)
        o = acc / l
        out = jnp.zeros((SUBLANES, FOX_DIM), F32)
        hl = _iota((SUBLANES, FOX_DIM), 1) // FOX_HD
        for h in range(FOX_HEADS):
            out = jnp.where(hl == h, o[h * SUBLANES:(h + 1) * SUBLANES, :], out)
        return out


def _fox_sample_kernel(pt_ref, q_ref, kn_ref, vn_ref, crow_ref, lfc_ref, ck_ref, cv_ref, o_ref,
                       kbuf, vbuf, lbuf, sem, *, n_seq, n_pages, layer):
    att = _PagedAttention(pt_ref, lfc_ref, ck_ref, cv_ref, kbuf, vbuf, lbuf, sem, n_seq, n_pages, layer)
    att.prime()

    def sequence(b, _):
        qbd, state = att.begin(q_ref[b])
        state = lax.fori_loop(0, att.n_groups, lambda gi, st: att.group(b * att.n_groups + gi, qbd, st), state)
        o_ref[b] = att.finish(qbd, state, kn_ref[b], vn_ref[b], crow_ref[b])
        return 0

    lax.fori_loop(0, n_seq, sequence, 0)


def _fox_sample(page_table, q, kn, vn, crow, lfc, ck, cv, layer):
    n_seq, n_pages = page_table.shape
    gw = FS_GROUP * PAGE_SIZE
    whole = lambda a: pl.BlockSpec(a.shape, lambda i, pt: (0,) * a.ndim)
    any_spec = pl.BlockSpec(memory_space=pl.ANY)
    grid_spec = pltpu.PrefetchScalarGridSpec(
        num_scalar_prefetch=1,
        grid=(1,),
        in_specs=[whole(q), whole(kn), whole(vn), whole(crow), any_spec, any_spec, any_spec],
        out_specs=pl.BlockSpec((n_seq, SUBLANES, FOX_DIM), lambda i, pt: (0, 0, 0)),
        scratch_shapes=[pltpu.VMEM((FS_SLOTS, FOX_DIM, gw), F32),
                        pltpu.VMEM((FS_SLOTS, FOX_DIM, gw), F32),
                        pltpu.VMEM((FS_SLOTS, FS_GROUP * FOX_HEADS, PAGE_SIZE), F32),
                        pltpu.SemaphoreType.DMA((3, FS_SLOTS))],
    )
    return pl.pallas_call(
        functools.partial(_fox_sample_kernel, n_seq=n_seq, n_pages=n_pages, layer=layer),
        grid_spec=grid_spec,
        out_shape=jax.ShapeDtypeStruct((n_seq, SUBLANES, FOX_DIM), F32),
        compiler_params=_cparams("arbitrary"),
        name="fox_sample",
    )(page_table, q, kn, vn, crow, lfc, ck, cv)


def _merge_kernel(ya_ref, yb_ref, yc_ref, zg_ref, x_ref, gb_ref, pa_ref, pb_ref, pc_ref, wo_ref, g_ref, b_ref, o_ref):
    merged = None
    for i, (y_ref, p_ref) in enumerate(((ya_ref, pa_ref), (yb_ref, pb_ref), (yc_ref, pc_ref))):
        cols = slice(i * D_MODEL, (i + 1) * D_MODEL)
        gate = jax.nn.sigmoid(zg_ref[:, cols] + gb_ref[:, cols])
        term = gate * _dot(y_ref[...].astype(BF16), p_ref[...])
        merged = term if merged is None else merged + term
    mix = _dot(merged.astype(BF16), wo_ref[...])
    o_ref[...] = _layer_norm(ALPHA * x_ref[...] + mix, g_ref[...], b_ref[...])


def _merge(ya, yb, yc, zg, x, gate_bias, pa, pb, pc, wo, g, b, layer, tm):
    t = x.shape[0]
    row = lambda n: pl.BlockSpec((tm, n), lambda i: (i, 0))
    consts = (gate_bias, pa, pb, pc, wo, g, b)
    return pl.pallas_call(
        _merge_kernel,
        grid=(t // tm,),
        in_specs=[row(D_A), row(SEG_BZ), row(FOX_DIM), row(N_BRANCH * D_MODEL), row(D_MODEL)]
        + [_layer_spec(c.shape, layer) for c in consts],
        out_specs=row(D_MODEL),
        out_shape=jax.ShapeDtypeStruct((t, D_MODEL), F32),
        compiler_params=_cparams("parallel"),
        name="merge",
    )(ya, yb, yc, zg, x, *consts)


_W_IN_K = 2 * D_A + CONV_DIM + 2 * GDN_HEADS + SEG_BZ + FOX_DIM


def _regroup_w_in(w_in):
    o_bqkv = 2 * D_A
    o_beta = o_bqkv + CONV_DIM
    o_a = o_beta + GDN_HEADS
    o_z = o_a + GDN_HEADS
    o_cqkv = o_z + SEG_BZ
    o_f = o_cqkv + 3 * FOX_DIM
    o_gate = o_f + FOX_HEADS
    small = jnp.concatenate([w_in[:, o_f:o_gate], w_in[:, o_beta:o_a], w_in[:, o_a:o_z],
                             jnp.zeros((D_MODEL, SEG_SMALL - FOX_HEADS - 2 * GDN_HEADS), w_in.dtype)], axis=1)
    return jnp.concatenate([w_in[:, :o_beta], w_in[:, o_z:o_cqkv], w_in[:, o_cqkv:o_f], w_in[:, o_gate:], small],
                           axis=1).astype(BF16)


def _lane_row(vals, offset):
    return jnp.zeros((1, LANES), F32).at[0, offset:offset + vals.shape[0]].set(vals.astype(F32))


def _layer_params(l, p):
    par = jnp.zeros((SUBLANES, LANES), F32)
    par = par.at[GDN_PAR_ALOG, SMALL_A:SMALL_A + GDN_HEADS].set(p["gdn_a_log"][l])
    par = par.at[GDN_PAR_DTB, SMALL_A:SMALL_A + GDN_HEADS].set(p["gdn_dt_bias"][l])
    par = par.at[GDN_PAR_NORM, :].set(p["gdn_norm_g"][l])
    conv_w = jnp.concatenate([p["gdn_conv_w"][l], jnp.zeros((SUBLANES - CONV_W, CONV_DIM), F32)], axis=0)
    row = lambda a: a[l].reshape(1, -1)
    return dict(
        w_in=_regroup_w_in(p["w_in"][l]),
        wkvt=p["w_in"][l][:, _W_IN_K:_W_IN_K + 2 * FOX_DIM].T.astype(BF16),
        sgu_ln=(row(p["sgu_ln_g"]), row(p["sgu_ln_b"])),
        sgu_w=p["sgu_w"][l], sgu_b=p["sgu_b"][l],
        f_bias=_lane_row(p["fox_f_bias"][l], SMALL_F),
        conv_w=conv_w, gdn_par=par,
    )


def _mixer_front(x, lp, n_seq, seq_rows, conv0, s0, long_seq, layer=0, kv_stack=None):
    if seq_rows >= SGU_CHUNK:
        sgu_w, sgu_b, span = lp["sgu_w"], lp["sgu_b"], SGU_CHUNK
    else:
        rep = SGU_CHUNK // seq_rows
        sgu_w = jnp.tile(lp["sgu_w"][:, :seq_rows, :seq_rows], (1, rep, rep))
        sgu_b = jnp.tile(lp["sgu_b"][:, :seq_rows], (1, rep))
        span = seq_rows
    proj = functools.partial(_in_proj, x, lp["w_in"], sgu_ln_g=lp["sgu_ln"][0], sgu_ln_b=lp["sgu_ln"][1], sgu_w=sgu_w,
                             sgu_b=sgu_b[:, :, None], f_bias=lp["f_bias"], n_seq=n_seq, seq_rows=seq_rows, sgu_span=span)
    if long_seq:
        ya, zq, zz, q, k, v, kb, vb, zg, zs, lf, crow = proj(wkvt=lp["wkvt"], layer=layer, kv_stack=kv_stack)
        vn = None
    else:
        ya, vn, zq, zz, q, k, v, zg, zs, lf, crow = proj(wkvt=None)
        kb = vb = None
    if long_seq:
        ns, nc, rows_in = 1, 4, 4 * GDN_CHUNK
    else:
        ns, nc, rows_in = 8, 1, seq_rows
    yb, s_fin = _gdn(zq, zs, zz, conv0, s0, lp["conv_w"], lp["gdn_par"], n_seq, seq_rows, ns, nc, rows_in,
                     BF16 if long_seq else F32)
    return dict(ya=ya, vn=vn, yb=yb, s_fin=s_fin, zq=zq, zg=zg, q=q, k=k, v=v, kb=kb, vb=vb, lf=lf, crow=crow)


def kernel(x_prompt, x_sample, cache_k, cache_v, cache_logf, page_table, state_conv, state_gdn,
           ln1_g, ln1_b, ffn1_w1, ffn1_w3, ffn1_w2, w_in, sgu_ln_g, sgu_ln_b, sgu_w, sgu_b,
           gdn_conv_w, gdn_a_log, gdn_dt_bias, gdn_norm_g, fox_f_bias, gate_bias,
           proj_a, proj_b, proj_c, w_out, ln2_g, ln2_b, ffn2_w1, ffn2_w3, ffn2_w2, ln3_g, ln3_b):
    p = dict(w_in=w_in, sgu_ln_g=sgu_ln_g, sgu_ln_b=sgu_ln_b, sgu_w=sgu_w, sgu_b=sgu_b, gdn_conv_w=gdn_conv_w,
             gdn_a_log=gdn_a_log, gdn_dt_bias=gdn_dt_bias, gdn_norm_g=gdn_norm_g, fox_f_bias=fox_f_bias)
    bp, sp, _ = x_prompt.shape
    db, ds, _ = x_sample.shape
    tp, ts = bp * sp, db * ds
    n_pool = cache_k.shape[1]
    tm_p = 1024 if tp % 1024 == 0 else 256
    tm_s = ts
    assert ts == PROJ_ROWS and sp % FOX_TQ == 0 and ds == SUBLANES and cache_k.shape[2] == PAGE_SIZE
    assert page_table.shape[1] % FS_GROUP == 0 and db % 8 == 0

    bf = lambda a: a.astype(BF16)
    rows3 = lambda a: a[:, None, :]
    ffn1 = (bf(ffn1_w1), bf(ffn1_w3), bf(ffn1_w2), rows3(ln1_g), rows3(ln1_b))
    ffn2 = (bf(ffn2_w1), bf(ffn2_w3), bf(ffn2_w2), rows3(ln3_g), rows3(ln3_b))
    mrg = (rows3(gate_bias), bf(proj_a), bf(proj_b), bf(proj_c), bf(w_out), rows3(ln2_g), rows3(ln2_b))

    yp = x_prompt.reshape(tp, D_MODEL)
    ys = x_sample.reshape(ts, D_MODEL)
    outs = [[] for _ in range(11)]
    kv_stack = None
    conv_pad = jnp.zeros((SUBLANES - (CONV_W - 1), CONV_DIM), F32)
    ck = jnp.transpose(cache_k, (0, 1, 3, 4, 2)).reshape(DEPTH, n_pool, FOX_DIM, PAGE_SIZE)
    cv = jnp.transpose(cache_v, (0, 1, 3, 4, 2)).reshape(DEPTH, n_pool, FOX_DIM, PAGE_SIZE)
    lfc = jnp.swapaxes(cache_logf, 2, 3)
    for l in range(DEPTH):
        lp = _layer_params(l, p)

        yp = _ffn_ln(yp, *ffn1, l, tm_p)
        conv0 = jnp.zeros((bp, SUBLANES, CONV_DIM), F32)
        s0 = jnp.zeros((bp, GDN_HEADS, GDN_DK, GDN_DV), F32)
        f = _mixer_front(yp, lp, bp, sp, conv0, s0, True, l, kv_stack)
        kv_stack = (f["k"], f["v"])
        nkv = sp // FOX_TK
        crow = f["crow"].reshape(bp, nkv, FOX_HEADS, FOX_TK)
        yc = _fox_prompt(f["q"], f["kb"], f["vb"], crow, bp, sp)
        yp = _merge(f["ya"], f["yb"], yc, f["zg"], yp, *mrg, l, min(tm_p, 512))
        yp = _ffn_ln(yp, *ffn2, l, tm_p)
        outs[2].append(jnp.transpose(f["lf"], (0, 2, 1)))
        outs[3].append(f["zq"].reshape(bp, sp, CONV_DIM)[:, sp - (CONV_W - 1):])
        outs[4].append(f["s_fin"])

        ys = _ffn_ln(ys, *ffn1, l, tm_s)
        conv0 = jnp.concatenate([jnp.broadcast_to(conv_pad, (db,) + conv_pad.shape), state_conv[l]], axis=1)
        f = _mixer_front(ys, lp, db, ds, conv0, state_gdn[l], False)
        crow = jnp.pad(f["crow"][0].reshape(FOX_HEADS, db, ds).transpose(1, 0, 2), ((0, 0), (0, 0), (0, LANES - ds)))
        per_seq = lambda a: a.reshape(db, ds, FOX_DIM)
        yc = _fox_sample(page_table, per_seq(f["q"]), per_seq(f["k"]), per_seq(f["v"]), crow, lfc, ck, cv, l)
        ys = _merge(f["ya"], f["yb"], yc.reshape(ts, FOX_DIM), f["zg"], ys, *mrg, l, tm_s)
        ys = _ffn_ln(ys, *ffn2, l, tm_s)
        outs[5].append(f["k"].reshape(db, ds, FOX_HEADS, FOX_HD))
        outs[6].append(f["v"].reshape(db, ds, FOX_HEADS, FOX_HD))
        outs[7].append(f["lf"][:, SMALL_F:SMALL_F + FOX_HEADS].reshape(db, ds, FOX_HEADS))
        outs[8].append(f["zq"].reshape(db, ds, CONV_DIM)[:, ds - (CONV_W - 1):])
        outs[9].append(f["s_fin"])
        outs[10].append(f["vn"].reshape(db, ds, D_A))

    to_bshd = lambda a: jnp.transpose(a.reshape(DEPTH, bp, FOX_HEADS, FOX_HD, sp), (0, 1, 4, 2, 3))
    stacked = [to_bshd(kv_stack[0]), to_bshd(kv_stack[1])] + [jnp.stack(o) for o in outs[2:]]
    return (yp.reshape(bp, sp, D_MODEL), ys.reshape(db, ds, D_MODEL)) + tuple(stacked)
```

```python
import functools

import jax
import jax.numpy as jnp
from jax import lax
from jax.experimental import pallas as pl
from jax.experimental.pallas import tpu as pltpu

F32 = jnp.float32
BF16 = jnp.bfloat16

D_MODEL = 1024
DEPTH = 2
PAGE_SIZE = 128
D_FF = ((8 * D_MODEL // 3 + 127) // 128) * 128
SGU_CHUNK = 128
SGU_GROUPS = 4
D_A = D_MODEL // 2
GDN_HEADS = 4
GDN_DK = 128
GDN_DV = 128
GDN_CHUNK = 64
CONV_W = 4
CONV_DIM = GDN_HEADS * (2 * GDN_DK + GDN_DV)
FOX_HEADS = 8
FOX_HD = 64
FOX_DIM = FOX_HEADS * FOX_HD
N_BRANCH = 3
ALPHA = (2.0 * DEPTH) ** 0.25
LN_EPS = 1e-5
NORM_EPS = 1e-6

LANES = 128
SUBLANES = 8
VMEM_LIMIT_BYTES = 56 * 1024 * 1024

SEG_A = 2 * D_A
SEG_BQKV = CONV_DIM
SEG_BZ = GDN_HEADS * GDN_DV
SEG_SMALL = LANES
SMALL_F = 0
SMALL_BETA = FOX_HEADS
SMALL_A = FOX_HEADS + GDN_HEADS

HIGHEST = lax.Precision.HIGHEST


def _cparams(*sem):
    return pltpu.CompilerParams(dimension_semantics=sem, vmem_limit_bytes=VMEM_LIMIT_BYTES)


def _const_spec(shape):
    nd = len(shape)
    return pl.BlockSpec(shape, lambda *_: (0,) * nd, pipeline_mode=pl.Buffered(1))


def _layer_spec(shape, layer):
    nd = len(shape) - 1
    return pl.BlockSpec((None,) + tuple(shape[1:]), lambda *_: (layer,) + (0,) * nd, pipeline_mode=pl.Buffered(1))


def _layer_norm(x, g, b):
    mu = jnp.mean(x, axis=-1, keepdims=True)
    xc = x - mu
    var = jnp.mean(xc * xc, axis=-1, keepdims=True)
    return xc * lax.rsqrt(var + LN_EPS) * g + b


def _softplus(x):
    return jnp.maximum(x, 0.0) + jnp.log1p(jnp.exp(-jnp.abs(x)))


def _log_sigmoid(x):
    return jnp.minimum(x, 0.0) - jnp.log1p(jnp.exp(-jnp.abs(x)))


def _dot(a, b):
    return jnp.dot(a, b, preferred_element_type=F32)


def _dot_nt(a, b):
    return lax.dot_general(a, b, (((1,), (1,)), ((), ())), preferred_element_type=F32)


def _bmm(a, b):
    return lax.dot_general(a.astype(BF16), b.astype(BF16), (((2,), (1,)), ((0,), (0,))), preferred_element_type=F32)


def _bmm_nt(a, b):
    return lax.dot_general(a.astype(BF16), b.astype(BF16), (((2,), (2,)), ((0,), (0,))), preferred_element_type=F32)


def _iota(shape, dim):
    return lax.broadcasted_iota(jnp.int32, shape, dim)


FFN_SPLITS = 11


def _ffn_ln_kernel(x_ref, w1_ref, w3_ref, w2_ref, g_ref, b_ref, o_ref):
    x = x_ref[...]
    xb = x.astype(BF16)
    fc = D_FF // FFN_SPLITS
    y = None
    for c in range(FFN_SPLITS):
        h1 = _dot(xb, w1_ref[:, c * fc:(c + 1) * fc])
        h3 = _dot(xb, w3_ref[:, c * fc:(c + 1) * fc])
        a = (h1 * jax.nn.sigmoid(h1) * h3).astype(BF16)
        part = _dot(a, w2_ref[c * fc:(c + 1) * fc, :])
        y = part if y is None else y + part
    o_ref[...] = _layer_norm(ALPHA * x + 0.5 * y, g_ref[...], b_ref[...])


def _ffn_ln(x, w1, w3, w2, g, b, layer, tm):
    t = x.shape[0]
    row = pl.BlockSpec((tm, D_MODEL), lambda i: (i, 0))
    return pl.pallas_call(
        _ffn_ln_kernel,
        grid=(t // tm,),
        in_specs=[row] + [_layer_spec(a.shape, layer) for a in (w1, w3, w2, g, b)],
        out_specs=row,
        out_shape=jax.ShapeDtypeStruct((t, D_MODEL), F32),
        compiler_params=_cparams("parallel"),
        name="ffn_ln",
    )(x, w1, w3, w2, g, b)


_IN_SEGS = (SEG_A, SEG_BQKV, SEG_BZ, FOX_DIM, FOX_DIM, FOX_DIM, N_BRANCH * D_MODEL, SEG_SMALL)
_IN_OFFS = tuple(sum(_IN_SEGS[:i]) for i in range(len(_IN_SEGS)))


PROJ_ROWS = 256


def _spatial_gate(za, ln_g, ln_b, w_ref, bs_ref, seq_rows):
    u = jax.nn.gelu(za[:, :D_A], approximate=True)
    vn = _layer_norm(jax.nn.gelu(za[:, D_A:], approximate=True), ln_g, ln_b)
    r = _iota((SGU_CHUNK, SGU_CHUNK), 0)
    c = _iota((SGU_CHUNK, SGU_CHUNK), 1)
    keep = (r >= c) & ((r // seq_rows) == (c // seq_rows))
    cg = D_A // SGU_GROUPS
    groups = []
    for g in range(SGU_GROUPS):
        wg = jnp.where(keep, w_ref[g], 0.0).astype(BF16)
        bg = bs_ref[g]
        cols = slice(g * cg, (g + 1) * cg)
        groups.append(jnp.concatenate(
            [_dot(wg, vn[n * SGU_CHUNK:(n + 1) * SGU_CHUNK, cols].astype(BF16)) + bg
             for n in range(za.shape[0] // SGU_CHUNK)], axis=0))
    return u * jnp.concatenate(groups, axis=1), vn


def _forget_sums(zs, f_bias, carry_ref, seq_rows):
    lf = _log_sigmoid(zs + f_bias)
    n = zs.shape[0]
    r = _iota((n, n), 0)
    c = _iota((n, n), 1)
    span = min(seq_rows, n)
    tri = jnp.where((r >= c) & ((r // span) == (c // span)), 1.0, 0.0).astype(F32)
    csum = jnp.dot(tri, lf, precision=HIGHEST, preferred_element_type=F32)
    if seq_rows > n:
        @pl.when(pl.program_id(0) % (seq_rows // n) == 0)
        def _():
            carry_ref[...] = jnp.zeros_like(carry_ref)

        csum = csum + carry_ref[0:1, :]
        carry_ref[...] = jnp.broadcast_to(csum[n - 1:n, :], carry_ref.shape)
    return lf, csum


def _in_proj_kernel(x_ref, w_ref, ln_g_ref, ln_b_ref, sw_ref, sb_ref, fb_ref, *refs, kv_transposed, n_stacked, seq_rows,
                    sgu_span):
    xb = x_ref[...].astype(BF16)

    def seg(i):
        return _dot(xb, w_ref[:, _IN_OFFS[i]:_IN_OFFS[i] + _IN_SEGS[i]])

    carry_ref = refs[-1]
    if kv_transposed:
        wkvt_ref = refs[0]
        ya_ref, zq_ref, zz_ref, q_ref, kt_ref, vt_ref, ktb_ref, vtb_ref, zg_ref, zs_ref, lf_ref, crow_ref = refs[1 + n_stacked:-1]
        kt = _dot_nt(wkvt_ref[0:FOX_DIM, :], xb)
        vt = _dot_nt(wkvt_ref[FOX_DIM:2 * FOX_DIM, :], xb)
        ktb_ref[...] = kt.astype(BF16)
        vtb_ref[...] = vt.astype(BF16)
        if n_stacked:
            kt_ref[...] = kt
            vt_ref[...] = vt
        else:
            for ref, val in ((kt_ref, kt), (vt_ref, vt)):
                ref[0] = val
                for other in range(1, ref.shape[0]):
                    ref[other] = jnp.zeros_like(val)
    else:
        ya_ref, vn_ref, zq_ref, zz_ref, q_ref, k_ref, v_ref, zg_ref, zs_ref, lf_ref, crow_ref = refs[:-1]
        k_ref[...] = seg(4)
        v_ref[...] = seg(5)
    ya, vn = _spatial_gate(seg(0), ln_g_ref[...], ln_b_ref[...], sw_ref, sb_ref, sgu_span)
    ya_ref[...] = ya.astype(ya_ref.dtype)
    if not kv_transposed:
        vn_ref[...] = vn
    zq_ref[...] = seg(1)
    zz_ref[...] = seg(2)
    q_ref[...] = (seg(3) * (FOX_HD ** -0.5)).astype(q_ref.dtype)
    zg_ref[...] = seg(6)
    zs = seg(7)
    zs_ref[...] = zs
    lf, csum = _forget_sums(zs, fb_ref[...], carry_ref, seq_rows)
    lf_ref[...] = lf.T[0:FOX_HEADS, :] if kv_transposed else lf
    crow_ref[0] = csum.T[0:FOX_HEADS, :]


def _in_proj(x, w, wkvt, sgu_ln_g, sgu_ln_b, sgu_w, sgu_b, f_bias, n_seq, seq_rows, sgu_span, layer=0, kv_stack=None):
    t = x.shape[0]
    tm = PROJ_ROWS
    nb = t // tm
    row = lambda n: pl.BlockSpec((tm, n), lambda i: (i, 0))
    sds = jax.ShapeDtypeStruct
    rows = lambda n, dt=F32: (row(n), sds((t, n), dt))
    consts = (w, sgu_ln_g, sgu_ln_b, sgu_w, sgu_b, f_bias)
    in_specs = [row(D_MODEL)] + [_const_spec(c.shape) for c in consts]
    operands = [x, *consts]
    crow = (pl.BlockSpec((1, FOX_HEADS, tm), lambda i: (i, 0, 0)), sds((nb, FOX_HEADS, tm), F32))
    tail = [rows(N_BRANCH * D_MODEL), rows(SEG_SMALL)]
    aliases = {}
    if wkvt is None:
        outs = [rows(D_A), rows(D_A), rows(SEG_BQKV), rows(SEG_BZ), rows(FOX_DIM), rows(FOX_DIM), rows(FOX_DIM)] + tail
        outs += [rows(LANES), crow]
    else:
        nt = seq_rows // tm
        if kv_stack is None:
            assert layer == 0
            kv_block = pl.BlockSpec((DEPTH, None, FOX_DIM, tm), lambda i: (0, i // nt, 0, i % nt))
        else:
            kv_block = pl.BlockSpec((None, None, FOX_DIM, tm), lambda i: (layer, i // nt, 0, i % nt))
        f32_kv = (kv_block, sds((DEPTH, n_seq, FOX_DIM, seq_rows), F32))
        bf_kv = (pl.BlockSpec((None, None, FOX_DIM, tm), lambda i: (i // nt, i % nt, 0, 0)),
                 sds((n_seq, nt, FOX_DIM, tm), BF16))
        lf = (pl.BlockSpec((None, FOX_HEADS, tm), lambda i: (i // nt, 0, i % nt)), sds((n_seq, FOX_HEADS, seq_rows), F32))
        outs = [rows(D_A, BF16), rows(SEG_BQKV), rows(SEG_BZ), rows(FOX_DIM, BF16), f32_kv, f32_kv, bf_kv, bf_kv] + tail
        outs += [lf, crow]
        in_specs.append(_const_spec(wkvt.shape))
        operands.append(wkvt)
        if kv_stack is not None:
            aliases = {len(operands): 4, len(operands) + 1: 5}
            in_specs += [pl.BlockSpec(memory_space=pl.ANY)] * 2
            operands += list(kv_stack)
    return pl.pallas_call(
        functools.partial(_in_proj_kernel, kv_transposed=wkvt is not None, n_stacked=len(aliases), seq_rows=seq_rows,
                          sgu_span=sgu_span),
        grid=(nb,),
        in_specs=in_specs,
        out_specs=[o[0] for o in outs],
        out_shape=[o[1] for o in outs],
        scratch_shapes=[pltpu.VMEM((SUBLANES, LANES), F32)],
        input_output_aliases=aliases,
        compiler_params=_cparams("arbitrary"),
        name="in_proj",
    )(*operands)


GDN_PAR_ALOG = 0
GDN_PAR_DTB = 1
GDN_PAR_NORM = 2


def _unit_lower_inverse_minus_identity(a):
    n = GDN_CHUNK
    r = _iota((1, n, n), 1)
    c = _iota((1, n, n), 2)
    same16 = (r // 16) == (c // 16)
    same32 = (r // 32) == (c // 32)
    ad = jnp.where(same16, a, 0.0)
    adb = ad.astype(BF16)
    t = -ad
    p = _bmm(adb, adb)
    for step in range(3):
        pb = p.astype(BF16)
        t = t + p + _bmm(-adb if step == 0 else t, pb)
        if step < 2:
            p = _bmm(pb, pb)
    for off in (jnp.where(same32 & ~same16, a, 0.0), jnp.where(~same32, a, 0.0)):
        tb = t.astype(BF16)
        m = off + _bmm(tb, off)
        t = t - m - _bmm(m, tb)
    return t


def _gdn_kernel(zq_ref, zs_ref, zz_ref, conv0_ref, s0_ref, cw_ref, par_ref, y_ref, sfin_ref, xp_scr, s_scr,
                *, ns, nc, rows_in, blocks_per_seq):
    n = GDN_CHUNK
    rows = nc * n
    nch = ns * nc
    blk = pl.program_id(1)

    @pl.when(blk == 0)
    def _():
        xp_scr[:, 0:SUBLANES, :] = conv0_ref[...]
        s_scr[...] = s0_ref[...].reshape(ns * GDN_HEADS, GDN_DK, GDN_DV)

    @pl.when(blk > 0)
    def _():
        xp_scr[:, 0:SUBLANES, :] = xp_scr[:, rows:rows + SUBLANES, :]

    xp_scr[:, SUBLANES:SUBLANES + rows_in, :] = zq_ref[...].reshape(ns, rows_in, CONV_DIM)
    if rows_in < rows:
        xp_scr[:, SUBLANES + rows_in:SUBLANES + rows, :] = jnp.zeros((ns, rows - rows_in, CONV_DIM), F32)

    base = SUBLANES - (CONV_W - 1)
    yc = xp_scr[:, base:base + rows, :] * cw_ref[0:1, :]
    for i in range(1, CONV_W):
        yc = yc + xp_scr[:, base + i:base + i + rows, :] * cw_ref[i:i + 1, :]
    yc = (yc * jax.nn.sigmoid(yc)).reshape(ns * rows, CONV_DIM)

    zs = zs_ref[...].reshape(ns, rows_in, LANES)
    if rows_in < rows:
        zs = jnp.concatenate([zs, jnp.zeros((ns, rows - rows_in, LANES), F32)], axis=1)
    valid = _iota((ns, rows, LANES), 1) < rows_in
    sig = jnp.where(valid, jax.nn.sigmoid(zs), 0.0).reshape(ns * rows, LANES)
    gl = -jnp.exp(par_ref[GDN_PAR_ALOG:GDN_PAR_ALOG + 1, :]) * _softplus(zs + par_ref[GDN_PAR_DTB:GDN_PAR_DTB + 1, :])
    gl = jnp.where(valid, gl, 0.0).reshape(ns * rows, LANES)

    rr = ns * rows
    r = _iota((rr, rr), 0)
    c = _iota((rr, rr), 1)
    tri = jnp.where((r >= c) & ((r // n) == (c // n)), 1.0, 0.0).astype(F32)
    gc = jnp.dot(tri, gl, precision=HIGHEST, preferred_element_type=F32)
    gct = gc.T

    order = [(cc, s, h) for cc in range(nc) for s in range(ns) for h in range(GDN_HEADS)]
    nb = len(order)

    def rows_of(cc, s):
        return slice(s * rows + cc * n, s * rows + (cc + 1) * n)

    def cols3(x, off, width):
        return jnp.stack([x[rows_of(cc, s), off + h * width:off + (h + 1) * width] for cc, s, h in order], axis=0)

    def l2_heads(off, scale):
        heads = []
        for h in range(GDN_HEADS):
            xh = yc[:, off + h * GDN_DK:off + (h + 1) * GDN_DK]
            heads.append(xh * (lax.rsqrt(jnp.sum(xh * xh, axis=-1, keepdims=True) + NORM_EPS) * scale))
        return jnp.concatenate(heads, axis=1)

    q = cols3(l2_heads(0, GDN_DK ** -0.5), 0, GDN_DK)
    k = cols3(l2_heads(GDN_HEADS * GDN_DK, 1.0), 0, GDN_DK)
    v = cols3(yc, 2 * GDN_HEADS * GDN_DK, GDN_DV)
    beta = cols3(sig, SMALL_BETA, 1)
    gcol = cols3(gc, SMALL_A, 1)
    grow = jnp.stack([gct[SMALL_A + h:SMALL_A + h + 1, rows_of(cc, s)] for cc, s, h in order], axis=0)
    glast = gcol[:, n - 1:n, :]

    ri = _iota((1, n, n), 1)
    ci = _iota((1, n, n), 2)
    decay = jnp.exp(jnp.where(ri >= ci, gcol - grow, -jnp.inf))
    egc = jnp.exp(gcol)
    qk_kk = _bmm_nt(jnp.concatenate([q, k], axis=1), k)
    qk = qk_kk[:, :n] * decay
    a = jnp.where(ri > ci, beta * qk_kk[:, n:] * decay, 0.0)
    tinv = _unit_lower_inverse_minus_identity(a)
    rhs = jnp.concatenate([v * beta, k * (beta * egc)], axis=-1)
    sol = rhs + _bmm(tinv, rhs)
    w_val = sol[:, :, :GDN_DV]
    k_cum = sol[:, :, GDN_DV:]
    q_dec = q * egc
    k_tail_t = jnp.swapaxes(k * jnp.exp(glast - gcol), 1, 2)
    g_last = jnp.exp(glast)

    norm_g = par_ref[GDN_PAR_NORM:GDN_PAR_NORM + 1, :]
    per = ns * GDN_HEADS
    for cc in range(nc):
        sl = slice(cc * per, (cc + 1) * per)
        s_old = s_scr[...]
        v_new = w_val[sl] - _bmm(k_cum[sl], s_old)
        o = _bmm(q_dec[sl], s_old) + _bmm(qk[sl], v_new)
        s_scr[...] = s_old * g_last[sl] + _bmm(k_tail_t[sl], v_new)
        nv = min(rows_in - cc * n, n)
        for s in range(ns):
            for h in range(GDN_HEADS):
                r0 = s * rows_in + cc * n
                cols = slice(h * GDN_DV, (h + 1) * GDN_DV)
                zz = zz_ref[r0:r0 + nv, cols]
                ov = o[s * GDN_HEADS + h, :nv]
                gate = zz * jax.nn.sigmoid(zz)
                y_ref[r0:r0 + nv, cols] = (ov * lax.rsqrt(jnp.mean(ov * ov, axis=-1, keepdims=True) + NORM_EPS)
                                           * norm_g * gate).astype(y_ref.dtype)

    @pl.when(blk == blocks_per_seq - 1)
    def _():
        sfin_ref[...] = s_scr[...].reshape(ns, GDN_HEADS, GDN_DK, GDN_DV)


def _gdn(zq, zs, zz, conv0, s0, conv_w, par, n_seq, seq_rows, ns, nc, rows_in, y_dtype):
    t = zq.shape[0]
    bps = seq_rows // rows_in
    rows = nc * GDN_CHUNK

    def row(ncol):
        return pl.BlockSpec((ns * rows_in, ncol), lambda b, j: (b * bps + j, 0))

    state = pl.BlockSpec((ns, GDN_HEADS, GDN_DK, GDN_DV), lambda b, j: (b, 0, 0, 0))
    return pl.pallas_call(
        functools.partial(_gdn_kernel, ns=ns, nc=nc, rows_in=rows_in, blocks_per_seq=bps),
        grid=(n_seq // ns, bps),
        in_specs=[row(CONV_DIM), row(LANES), row(SEG_BZ),
                  pl.BlockSpec((ns, SUBLANES, CONV_DIM), lambda b, j: (b, 0, 0)), state,
                  _const_spec(conv_w.shape), _const_spec(par.shape)],
        out_specs=[row(SEG_BZ), state],
        out_shape=[jax.ShapeDtypeStruct((t, SEG_BZ), y_dtype),
                   jax.ShapeDtypeStruct((n_seq, GDN_HEADS, GDN_DK, GDN_DV), F32)],
        scratch_shapes=[pltpu.VMEM((ns, rows + 2 * SUBLANES, CONV_DIM), F32),
                        pltpu.VMEM((ns * GDN_HEADS, GDN_DK, GDN_DV), F32)],
        compiler_params=_cparams("parallel", "arbitrary"),
        name="gdn",
    )(zq, zs, zz, conv0, s0, conv_w, par)


FOX_TQ = 2048
FOX_SUB = 128
FOX_TK = PROJ_ROWS


def _fox_prompt_kernel(qb_ref, kt_ref, vt_ref, crow_ref, o_ref, *, nq):
    hp = pl.program_id(1)
    nsub = FOX_TQ // FOX_SUB
    below = _iota((FOX_SUB, FOX_TQ), 0) - _iota((FOX_SUB, FOX_TQ), 1)
    ones = jnp.ones((FOX_HD, FOX_TK), BF16)

    def q_block(qi, _):
        r0 = pl.multiple_of(qi * FOX_TQ, FOX_TQ)
        qs = [[qb_ref[pl.ds(r0 + sb * FOX_SUB, FOX_SUB), e * FOX_HD:(e + 1) * FOX_HD] for sb in range(nsub)]
              for e in range(2)]

        per_q = FOX_TQ // FOX_TK

        def step(j0, carry, diagonal):
            out = list(carry)
            cat = lambda xs: xs[0] if len(xs) == 1 else jnp.concatenate(xs, axis=1)
            heads = lambda ref, u, e: ref[j0 + u, e * FOX_HD:(e + 1) * FOX_HD, :]
            kt = [[heads(kt_ref, u, e) for u in range(per_q)] for e in range(2)]
            vt1 = [[jnp.concatenate([heads(vt_ref, u, e), ones], axis=0) for u in range(per_q)] for e in range(2)]
            nrow = [[-crow_ref[j0 + u, pl.ds(2 * hp + e, 1), :] for u in range(per_q)] for e in range(2)]
            seen = lambda sb: per_q if not diagonal else ((sb + 1) * FOX_SUB - 1) // FOX_TK + 1
            chains = [(e, sb) for e in range(2) for sb in range(nsub)]
            s = {}
            for e, sb in chains:
                n = seen(sb)
                s[e, sb] = _dot(qs[e][sb], cat(kt[e][:n])) + cat(nrow[e][:n])
                if diagonal:
                    s[e, sb] = jnp.where(below[:, :n * FOX_TK] + sb * FOX_SUB >= 0, s[e, sb], -jnp.inf)
            m_new = {c: jnp.maximum(carry[c[0] * nsub + c[1]][0], jnp.max(s[c], axis=-1, keepdims=True)) for c in chains}
            p = {c: jnp.exp(s[c] - m_new[c]).astype(BF16) for c in chains}
            for e, sb in chains:
                m, acc = carry[e * nsub + sb]
                acc = jnp.exp(m - m_new[e, sb]) * acc + _dot_nt(p[e, sb], cat(vt1[e][:seen(sb)]))
                out[e * nsub + sb] = (m_new[e, sb], acc)
            return tuple(out)

        init = tuple((jnp.full((FOX_SUB, 1), -jnp.inf, F32), jnp.zeros((FOX_SUB, 2 * FOX_HD), F32))
                     for _ in range(2 * nsub))
        carry = lax.fori_loop(0, qi, lambda i, cr: step(i * per_q, cr, False), init)
        carry = step(qi * per_q, carry, True)
        for sb in range(nsub):
            acc = [carry[e * nsub + sb][1] for e in range(2)]
            o_ref[pl.ds(r0 + sb * FOX_SUB, FOX_SUB), :] = jnp.concatenate(
                [a[:, :FOX_HD] / a[:, FOX_HD:] for a in acc], axis=-1).astype(o_ref.dtype)
        return 0

    lax.fori_loop(0, nq, q_block, 0)


def _fox_prompt(qb, ktb, vtb, crow, n_seq, seq_rows):
    t = qb.shape[0]
    nkv = seq_rows // FOX_TK
    qspec = pl.BlockSpec((seq_rows, LANES), lambda b, h: (b, h))
    kvspec = pl.BlockSpec((None, nkv, LANES, FOX_TK), lambda b, h: (b, 0, h, 0))
    return pl.pallas_call(
        functools.partial(_fox_prompt_kernel, nq=seq_rows // FOX_TQ),
        grid=(n_seq, FOX_HEADS // 2),
        in_specs=[qspec, kvspec, kvspec,
                  pl.BlockSpec((None, nkv, FOX_HEADS, FOX_TK), lambda b, h: (b, 0, 0, 0))],
        out_specs=qspec,
        out_shape=jax.ShapeDtypeStruct((t, FOX_DIM), BF16),
        compiler_params=_cparams("parallel", "parallel"),
        name="fox_prompt",
    )(qb, ktb, vtb, crow)


FS_GROUP = 8
FS_SLOTS = 3
FS_ROWS = FOX_HEADS * SUBLANES


class _PagedAttention:
    def __init__(self, pt_ref, lfc_ref, ck_ref, cv_ref, kbuf, vbuf, lbuf, sem, n_seq, n_pages, layer):
        self.pt_ref, self.lfc_ref, self.ck_ref, self.cv_ref = pt_ref, lfc_ref, ck_ref, cv_ref
        self.kbuf, self.vbuf, self.lbuf, self.sem = kbuf, vbuf, lbuf, sem
        self.n_pages, self.layer = n_pages, layer
        self.n_groups = n_pages // FS_GROUP
        self.total = n_seq * self.n_groups
        rr = _iota((FS_ROWS, FOX_DIM), 0)
        cc = _iota((FS_ROWS, FOX_DIM), 1)
        self.own = (rr // SUBLANES) == (cc // FOX_HD)
        pr = _iota((PAGE_SIZE, PAGE_SIZE), 0)
        pc = _iota((PAGE_SIZE, PAGE_SIZE), 1)
        self.later = jnp.where(pr > pc, 1.0, 0.0).astype(F32)

    def copies(self, f, slot):
        b = f // self.n_groups
        gi = f % self.n_groups
        out = []
        for u in range(FS_GROUP):
            page = self.pt_ref[b, self.n_pages - 1 - (gi * FS_GROUP + u)]
            lanes = slice(u * PAGE_SIZE, (u + 1) * PAGE_SIZE)
            heads = slice(u * FOX_HEADS, (u + 1) * FOX_HEADS)
            out.append(pltpu.make_async_copy(self.ck_ref.at[self.layer, page], self.kbuf.at[slot, :, lanes],
                                             self.sem.at[0, slot]))
            out.append(pltpu.make_async_copy(self.cv_ref.at[self.layer, page], self.vbuf.at[slot, :, lanes],
                                             self.sem.at[1, slot]))
            out.append(pltpu.make_async_copy(self.lfc_ref.at[self.layer, page], self.lbuf.at[slot, heads, :],
                                             self.sem.at[2, slot]))
        return out

    def prime(self):
        for f0 in range(FS_SLOTS - 1):
            for cp in self.copies(f0, f0):
                cp.start()

    @staticmethod
    def rows8(x):
        return jnp.concatenate([jnp.broadcast_to(x[h:h + 1, :], (SUBLANES, x.shape[1])) for h in range(FOX_HEADS)], axis=0)

    @staticmethod
    def update(carry, s, vbf, v_transposed):
        m, l, acc = carry
        m_new = jnp.maximum(m, jnp.max(s, axis=-1, keepdims=True))
        alpha = jnp.exp(m - m_new)
        p = jnp.exp(s - m_new)
        l = alpha * l + jnp.sum(p, axis=-1, keepdims=True)
        pv = _dot_nt(p.astype(BF16), vbf) if v_transposed else _dot(p.astype(BF16), vbf)
        return m_new, l, alpha * acc + pv

    def begin(self, q):
        qrep = jnp.concatenate([q.astype(BF16)] * FOX_HEADS, axis=0)
        qbd = jnp.where(self.own, qrep, jnp.zeros_like(qrep))
        state = (jnp.full((FS_ROWS, 1), -jnp.inf, F32), jnp.zeros((FS_ROWS, 1), F32),
                 jnp.zeros((FS_ROWS, FOX_DIM), F32), jnp.zeros((FOX_HEADS, 1), F32))
        return qbd, state

    def group(self, f, qbd, state):
        m, l, acc, tail = state
        slot = f % FS_SLOTS
        nxt = f + FS_SLOTS - 1

        @pl.when(nxt < self.total)
        def _():
            for cp in self.copies(nxt, nxt % FS_SLOTS):
                cp.start()

        for cp in self.copies(f, slot):
            cp.wait()
        lf = self.lbuf[slot]
        suffix = jnp.dot(lf, self.later, precision=HIGHEST, preferred_element_type=F32)
        tot = jnp.sum(lf, axis=-1, keepdims=True)
        bias = []
        for u in range(FS_GROUP):
            rows = slice(u * FOX_HEADS, (u + 1) * FOX_HEADS)
            bias.append(self.rows8(suffix[rows] + tail))
            tail = tail + tot[rows]
        s = _dot(qbd, self.kbuf[slot].astype(BF16)) + jnp.concatenate(bias, axis=1)
        m, l, acc = self.update((m, l, acc), s, self.vbuf[slot].astype(BF16), True)
        return m, l, acc, tail

    def finish(self, qbd, state, kn, vn, crow):
        m, l, acc, _ = state
        pad = jnp.zeros((PAGE_SIZE - SUBLANES, FOX_DIM), F32)
        knp = jnp.concatenate([kn, pad], axis=0).astype(BF16)
        vnp = jnp.concatenate([vn, pad], axis=0).astype(BF16)
        tok = _iota((FS_ROWS, PAGE_SIZE), 0) % SUBLANES
        pos = _iota((FS_ROWS, PAGE_SIZE), 1)
        s = jnp.where(pos <= tok, _dot_nt(qbd, knp) - self.rows8(crow), -jnp.inf)
        m, l, acc = self.update((m, l, acc), s, vnp, False)
        o = acc / l
        out = jnp.zeros((SUBLANES, FOX_DIM), F32)
        hl = _iota((SUBLANES, FOX_DIM), 1) // FOX_HD
        for h in range(FOX_HEADS):
            out = jnp.where(hl == h, o[h * SUBLANES:(h + 1) * SUBLANES, :], out)
        return out


def _fox_sample_kernel(pt_ref, q_ref, kn_ref, vn_ref, crow_ref, lfc_ref, ck_ref, cv_ref, o_ref,
                       kbuf, vbuf, lbuf, sem, *, n_seq, n_pages, layer):
    att = _PagedAttention(pt_ref, lfc_ref, ck_ref, cv_ref, kbuf, vbuf, lbuf, sem, n_seq, n_pages, layer)
    att.prime()

    def sequence(b, _):
        qbd, state = att.begin(q_ref[b])
        state = lax.fori_loop(0, att.n_groups, lambda gi, st: att.group(b * att.n_groups + gi, qbd, st), state)
        o_ref[b] = att.finish(qbd, state, kn_ref[b], vn_ref[b], crow_ref[b])
        return 0

    lax.fori_loop(0, n_seq, sequence, 0)


def _fox_sample(page_table, q, kn, vn, crow, lfc, ck, cv, layer):
    n_seq, n_pages = page_table.shape
    gw = FS_GROUP * PAGE_SIZE
    whole = lambda a: pl.BlockSpec(a.shape, lambda i, pt: (0,) * a.ndim)
    any_spec = pl.BlockSpec(memory_space=pl.ANY)
    grid_spec = pltpu.PrefetchScalarGridSpec(
        num_scalar_prefetch=1,
        grid=(1,),
        in_specs=[whole(q), whole(kn), whole(vn), whole(crow), any_spec, any_spec, any_spec],
        out_specs=pl.BlockSpec((n_seq, SUBLANES, FOX_DIM), lambda i, pt: (0, 0, 0)),
        scratch_shapes=[pltpu.VMEM((FS_SLOTS, FOX_DIM, gw), F32),
                        pltpu.VMEM((FS_SLOTS, FOX_DIM, gw), F32),
                        pltpu.VMEM((FS_SLOTS, FS_GROUP * FOX_HEADS, PAGE_SIZE), F32),
                        pltpu.SemaphoreType.DMA((3, FS_SLOTS))],
    )
    return pl.pallas_call(
        functools.partial(_fox_sample_kernel, n_seq=n_seq, n_pages=n_pages, layer=layer),
        grid_spec=grid_spec,
        out_shape=jax.ShapeDtypeStruct((n_seq, SUBLANES, FOX_DIM), F32),
        compiler_params=_cparams("arbitrary"),
        name="fox_sample",
    )(page_table, q, kn, vn, crow, lfc, ck, cv)


def _merge_kernel(ya_ref, yb_ref, yc_ref, zg_ref, x_ref, gb_ref, pa_ref, pb_ref, pc_ref, wo_ref, g_ref, b_ref, o_ref):
    merged = None
    for i, (y_ref, p_ref) in enumerate(((ya_ref, pa_ref), (yb_ref, pb_ref), (yc_ref, pc_ref))):
        cols = slice(i * D_MODEL, (i + 1) * D_MODEL)
        gate = jax.nn.sigmoid(zg_ref[:, cols] + gb_ref[:, cols])
        term = gate * _dot(y_ref[...].astype(BF16), p_ref[...])
        merged = term if merged is None else merged + term
    mix = _dot(merged.astype(BF16), wo_ref[...])
    o_ref[...] = _layer_norm(ALPHA * x_ref[...] + mix, g_ref[...], b_ref[...])


def _merge(ya, yb, yc, zg, x, gate_bias, pa, pb, pc, wo, g, b, layer, tm):
    t = x.shape[0]
    row = lambda n: pl.BlockSpec((tm, n), lambda i: (i, 0))
    consts = (gate_bias, pa, pb, pc, wo, g, b)
    return pl.pallas_call(
        _merge_kernel,
        grid=(t // tm,),
        in_specs=[row(D_A), row(SEG_BZ), row(FOX_DIM), row(N_BRANCH * D_MODEL), row(D_MODEL)]
        + [_layer_spec(c.shape, layer) for c in consts],
        out_specs=row(D_MODEL),
        out_shape=jax.ShapeDtypeStruct((t, D_MODEL), F32),
        compiler_params=_cparams("parallel"),
        name="merge",
    )(ya, yb, yc, zg, x, *consts)


_W_IN_K = 2 * D_A + CONV_DIM + 2 * GDN_HEADS + SEG_BZ + FOX_DIM


def _regroup_w_in(w_in):
    o_bqkv = 2 * D_A
    o_beta = o_bqkv + CONV_DIM
    o_a = o_beta + GDN_HEADS
    o_z = o_a + GDN_HEADS
    o_cqkv = o_z + SEG_BZ
    o_f = o_cqkv + 3 * FOX_DIM
    o_gate = o_f + FOX_HEADS
    small = jnp.concatenate([w_in[:, o_f:o_gate], w_in[:, o_beta:o_a], w_in[:, o_a:o_z],
                             jnp.zeros((D_MODEL, SEG_SMALL - FOX_HEADS - 2 * GDN_HEADS), w_in.dtype)], axis=1)
    return jnp.concatenate([w_in[:, :o_beta], w_in[:, o_z:o_cqkv], w_in[:, o_cqkv:o_f], w_in[:, o_gate:], small],
                           axis=1).astype(BF16)


def _lane_row(vals, offset):
    return jnp.zeros((1, LANES), F32).at[0, offset:offset + vals.shape[0]].set(vals.astype(F32))


def _layer_params(l, p):
    par = jnp.zeros((SUBLANES, LANES), F32)
    par = par.at[GDN_PAR_ALOG, SMALL_A:SMALL_A + GDN_HEADS].set(p["gdn_a_log"][l])
    par = par.at[GDN_PAR_DTB, SMALL_A:SMALL_A + GDN_HEADS].set(p["gdn_dt_bias"][l])
    par = par.at[GDN_PAR_NORM, :].set(p["gdn_norm_g"][l])
    conv_w = jnp.concatenate([p["gdn_conv_w"][l], jnp.zeros((SUBLANES - CONV_W, CONV_DIM), F32)], axis=0)
    row = lambda a: a[l].reshape(1, -1)
    return dict(
        w_in=_regroup_w_in(p["w_in"][l]),
        wkvt=p["w_in"][l][:, _W_IN_K:_W_IN_K + 2 * FOX_DIM].T.astype(BF16),
        sgu_ln=(row(p["sgu_ln_g"]), row(p["sgu_ln_b"])),
        sgu_w=p["sgu_w"][l], sgu_b=p["sgu_b"][l],
        f_bias=_lane_row(p["fox_f_bias"][l], SMALL_F),
        conv_w=conv_w, gdn_par=par,
    )


def _mixer_front(x, lp, n_seq, seq_rows, conv0, s0, long_seq, layer=0, kv_stack=None):
    if seq_rows >= SGU_CHUNK:
        sgu_w, sgu_b, span = lp["sgu_w"], lp["sgu_b"], SGU_CHUNK
    else:
        rep = SGU_CHUNK // seq_rows
        sgu_w = jnp.tile(lp["sgu_w"][:, :seq_rows, :seq_rows], (1, rep, rep))
        sgu_b = jnp.tile(lp["sgu_b"][:, :seq_rows], (1, rep))
        span = seq_rows
    proj = functools.partial(_in_proj, x, lp["w_in"], sgu_ln_g=lp["sgu_ln"][0], sgu_ln_b=lp["sgu_ln"][1], sgu_w=sgu_w,
                             sgu_b=sgu_b[:, :, None], f_bias=lp["f_bias"], n_seq=n_seq, seq_rows=seq_rows, sgu_span=span)
    if long_seq:
        ya, zq, zz, q, k, v, kb, vb, zg, zs, lf, crow = proj(wkvt=lp["wkvt"], layer=layer, kv_stack=kv_stack)
        vn = None
    else:
        ya, vn, zq, zz, q, k, v, zg, zs, lf, crow = proj(wkvt=None)
        kb = vb = None
    if long_seq:
        ns, nc, rows_in = 1, 4, 4 * GDN_CHUNK
    else:
        ns, nc, rows_in = 8, 1, seq_rows
    yb, s_fin = _gdn(zq, zs, zz, conv0, s0, lp["conv_w"], lp["gdn_par"], n_seq, seq_rows, ns, nc, rows_in,
                     BF16 if long_seq else F32)
    return dict(ya=ya, vn=vn, yb=yb, s_fin=s_fin, zq=zq, zg=zg, q=q, k=k, v=v, kb=kb, vb=vb, lf=lf, crow=crow)


def kernel(x_prompt, x_sample, cache_k, cache_v, cache_logf, page_table, state_conv, state_gdn,
           ln1_g, ln1_b, ffn1_w1, ffn1_w3, ffn1_w2, w_in, sgu_ln_g, sgu_ln_b, sgu_w, sgu_b,
           gdn_conv_w, gdn_a_log, gdn_dt_bias, gdn_norm_g, fox_f_bias, gate_bias,
           proj_a, proj_b, proj_c, w_out, ln2_g, ln2_b, ffn2_w1, ffn2_w3, ffn2_w2, ln3_g, ln3_b):
    p = dict(w_in=w_in, sgu_ln_g=sgu_ln_g, sgu_ln_b=sgu_ln_b, sgu_w=sgu_w, sgu_b=sgu_b, gdn_conv_w=gdn_conv_w,
             gdn_a_log=gdn_a_log, gdn_dt_bias=gdn_dt_bias, gdn_norm_g=gdn_norm_g, fox_f_bias=fox_f_bias)
    bp, sp, _ = x_prompt.shape
    db, ds, _ = x_sample.shape
    tp, ts = bp * sp, db * ds
    n_pool = cache_k.shape[1]
    tm_p = 1024 if tp % 1024 == 0 else 256
    tm_s = ts
    assert ts == PROJ_ROWS and sp % FOX_TQ == 0 and ds == SUBLANES and cache_k.shape[2] == PAGE_SIZE
    assert page_table.shape[1] % FS_GROUP == 0 and db % 8 == 0

    bf = lambda a: a.astype(BF16)
    rows3 = lambda a: a[:, None, :]
    ffn1 = (bf(ffn1_w1), bf(ffn1_w3), bf(ffn1_w2), rows3(ln1_g), rows3(ln1_b))
    ffn2 = (bf(ffn2_w1), bf(ffn2_w3), bf(ffn2_w2), rows3(ln3_g), rows3(ln3_b))
    mrg = (rows3(gate_bias), bf(proj_a), bf(proj_b), bf(proj_c), bf(w_out), rows3(ln2_g), rows3(ln2_b))

    yp = x_prompt.reshape(tp, D_MODEL)
    ys = x_sample.reshape(ts, D_MODEL)
    outs = [[] for _ in range(11)]
    kv_stack = None
    conv_pad = jnp.zeros((SUBLANES - (CONV_W - 1), CONV_DIM), F32)
    ck = jnp.transpose(cache_k, (0, 1, 3, 4, 2)).reshape(DEPTH, n_pool, FOX_DIM, PAGE_SIZE)
    cv = jnp.transpose(cache_v, (0, 1, 3, 4, 2)).reshape(DEPTH, n_pool, FOX_DIM, PAGE_SIZE)
    lfc = jnp.swapaxes(cache_logf, 2, 3)
    for l in range(DEPTH):
        lp = _layer_params(l, p)

        yp = _ffn_ln(yp, *ffn1, l, tm_p)
        conv0 = jnp.zeros((bp, SUBLANES, CONV_DIM), F32)
        s0 = jnp.zeros((bp, GDN_HEADS, GDN_DK, GDN_DV), F32)
        f = _mixer_front(yp, lp, bp, sp, conv0, s0, True, l, kv_stack)
        kv_stack = (f["k"], f["v"])
        nkv = sp // FOX_TK
        crow = f["crow"].reshape(bp, nkv, FOX_HEADS, FOX_TK)
        yc = _fox_prompt(f["q"], f["kb"], f["vb"], crow, bp, sp)
        yp = _merge(f["ya"], f["yb"], yc, f["zg"], yp, *mrg, l, min(tm_p, 512))
        yp = _ffn_ln(yp, *ffn2, l, tm_p)
        outs[2].append(jnp.transpose(f["lf"], (0, 2, 1)))
        outs[3].append(f["zq"].reshape(bp, sp, CONV_DIM)[:, sp - (CONV_W - 1):])
        outs[4].append(f["s_fin"])

        ys = _ffn_ln(ys, *ffn1, l, tm_s)
        conv0 = jnp.concatenate([jnp.broadcast_to(conv_pad, (db,) + conv_pad.shape), state_conv[l]], axis=1)
        f = _mixer_front(ys, lp, db, ds, conv0, state_gdn[l], False)
        crow = jnp.pad(f["crow"][0].reshape(FOX_HEADS, db, ds).transpose(1, 0, 2), ((0, 0), (0, 0), (0, LANES - ds)))
        per_seq = lambda a: a.reshape(db, ds, FOX_DIM)
        yc = _fox_sample(page_table, per_seq(f["q"]), per_seq(f["k"]), per_seq(f["v"]), crow, lfc, ck, cv, l)
        ys = _merge(f["ya"], f["yb"], yc.reshape(ts, FOX_DIM), f["zg"], ys, *mrg, l, tm_s)
        ys = _ffn_ln(ys, *ffn2, l, tm_s)
        outs[5].append(f["k"].reshape(db, ds, FOX_HEADS, FOX_HD))
        outs[6].append(f["v"].reshape(db, ds, FOX_HEADS, FOX_HD))
        outs[7].append(f["lf"][:, SMALL_F:SMALL_F + FOX_HEADS].reshape(db, ds, FOX_HEADS))
        outs[8].append(f["zq"].reshape(db, ds, CONV_DIM)[:, ds - (CONV_W - 1):])
        outs[9].append(f["s_fin"])
        outs[10].append(f["vn"].reshape(db, ds, D_A))

    to_bshd = lambda a: jnp.transpose(a.reshape(DEPTH, bp, FOX_HEADS, FOX_HD, sp), (0, 1, 4, 2, 3))
    stacked = [to_bshd(kv_stack[0]), to_bshd(kv_stack[1])] + [jnp.stack(o) for o in outs[2:]]
    return (yp.reshape(bp, sp, D_MODEL), ys.reshape(db, ds, D_MODEL)) + tuple(stacked)
```

```python
import functools

import jax
import jax.numpy as jnp
from jax import lax
from jax.experimental import pallas as pl
from jax.experimental.pallas import tpu as pltpu

F32 = jnp.float32
BF16 = jnp.bfloat16

D_MODEL = 1024
DEPTH = 2
PAGE_SIZE = 128
D_FF = ((8 * D_MODEL // 3 + 127) // 128) * 128
SGU_CHUNK = 128
SGU_GROUPS = 4
D_A = D_MODEL // 2
GDN_HEADS = 4
GDN_DK = 128
GDN_DV = 128
GDN_CHUNK = 64
CONV_W = 4
CONV_DIM = GDN_HEADS * (2 * GDN_DK + GDN_DV)
FOX_HEADS = 8
FOX_HD = 64
FOX_DIM = FOX_HEADS * FOX_HD
N_BRANCH = 3
ALPHA = (2.0 * DEPTH) ** 0.25
LN_EPS = 1e-5
NORM_EPS = 1e-6

LANES = 128
SUBLANES = 8
VMEM_LIMIT_BYTES = 56 * 1024 * 1024

SEG_A = 2 * D_A
SEG_BQKV = CONV_DIM
SEG_BZ = GDN_HEADS * GDN_DV
SEG_SMALL = LANES
SMALL_F = 0
SMALL_BETA = FOX_HEADS
SMALL_A = FOX_HEADS + GDN_HEADS

HIGHEST = lax.Precision.HIGHEST


def _cparams(*sem):
    return pltpu.CompilerParams(dimension_semantics=sem, vmem_limit_bytes=VMEM_LIMIT_BYTES)


def _const_spec(shape):
    nd = len(shape)
    return pl.BlockSpec(shape, lambda *_: (0,) * nd, pipeline_mode=pl.Buffered(1))


def _layer_spec(shape, layer):
    nd = len(shape) - 1
    return pl.BlockSpec((None,) + tuple(shape[1:]), lambda *_: (layer,) + (0,) * nd, pipeline_mode=pl.Buffered(1))


def _layer_norm(x, g, b):
    mu = jnp.mean(x, axis=-1, keepdims=True)
    xc = x - mu
    var = jnp.mean(xc * xc, axis=-1, keepdims=True)
    return xc * lax.rsqrt(var + LN_EPS) * g + b


def _softplus(x):
    return jnp.maximum(x, 0.0) + jnp.log1p(jnp.exp(-jnp.abs(x)))


def _log_sigmoid(x):
    return jnp.minimum(x, 0.0) - jnp.log1p(jnp.exp(-jnp.abs(x)))


def _dot(a, b):
    return jnp.dot(a, b, preferred_element_type=F32)


def _dot_nt(a, b):
    return lax.dot_general(a, b, (((1,), (1,)), ((), ())), preferred_element_type=F32)


def _bmm(a, b):
    return lax.dot_general(a.astype(BF16), b.astype(BF16), (((2,), (1,)), ((0,), (0,))), preferred_element_type=F32)


def _bmm_nt(a, b):
    return lax.dot_general(a.astype(BF16), b.astype(BF16), (((2,), (2,)), ((0,), (0,))), preferred_element_type=F32)


def _iota(shape, dim):
    return lax.broadcasted_iota(jnp.int32, shape, dim)


FFN_SPLITS = 11


def _ffn_ln_kernel(x_ref, w1_ref, w3_ref, w2_ref, g_ref, b_ref, o_ref):
    x = x_ref[...]
    xb = x.astype(BF16)
    fc = D_FF // FFN_SPLITS
    y = None
    for c in range(FFN_SPLITS):
        h1 = _dot(xb, w1_ref[:, c * fc:(c + 1) * fc])
        h3 = _dot(xb, w3_ref[:, c * fc:(c + 1) * fc])
        a = (h1 * jax.nn.sigmoid(h1) * h3).astype(BF16)
        part = _dot(a, w2_ref[c * fc:(c + 1) * fc, :])
        y = part if y is None else y + part
    o_ref[...] = _layer_norm(ALPHA * x + 0.5 * y, g_ref[...], b_ref[...])


def _ffn_ln(x, w1, w3, w2, g, b, layer, tm):
    t = x.shape[0]
    row = pl.BlockSpec((tm, D_MODEL), lambda i: (i, 0))
    return pl.pallas_call(
        _ffn_ln_kernel,
        grid=(t // tm,),
        in_specs=[row] + [_layer_spec(a.shape, layer) for a in (w1, w3, w2, g, b)],
        out_specs=row,
        out_shape=jax.ShapeDtypeStruct((t, D_MODEL), F32),
        compiler_params=_cparams("parallel"),
        name="ffn_ln",
    )(x, w1, w3, w2, g, b)


_IN_SEGS = (SEG_A, SEG_BQKV, SEG_BZ, FOX_DIM, FOX_DIM, FOX_DIM, N_BRANCH * D_MODEL, SEG_SMALL)
_IN_OFFS = tuple(sum(_IN_SEGS[:i]) for i in range(len(_IN_SEGS)))


PROJ_ROWS = 256


def _spatial_gate(za, ln_g, ln_b, w_ref, bs_ref, seq_rows):
    u = jax.nn.gelu(za[:, :D_A], approximate=True)
    vn = _layer_norm(jax.nn.gelu(za[:, D_A:], approximate=True), ln_g, ln_b)
    r = _iota((SGU_CHUNK, SGU_CHUNK), 0)
    c = _iota((SGU_CHUNK, SGU_CHUNK), 1)
    keep = (r >= c) & ((r // seq_rows) == (c // seq_rows))
    cg = D_A // SGU_GROUPS
    groups = []
    for g in range(SGU_GROUPS):
        wg = jnp.where(keep, w_ref[g], 0.0).astype(BF16)
        bg = bs_ref[g]
        cols = slice(g * cg, (g + 1) * cg)
        groups.append(jnp.concatenate(
            [_dot(wg, vn[n * SGU_CHUNK:(n + 1) * SGU_CHUNK, cols].astype(BF16)) + bg
             for n in range(za.shape[0] // SGU_CHUNK)], axis=0))
    return u * jnp.concatenate(groups, axis=1), vn


def _forget_sums(zs, f_bias, carry_ref, seq_rows):
    lf = _log_sigmoid(zs + f_bias)
    n = zs.shape[0]
    r = _iota((n, n), 0)
    c = _iota((n, n), 1)
    span = min(seq_rows, n)
    tri = jnp.where((r >= c) & ((r // span) == (c // span)), 1.0, 0.0).astype(F32)
    csum = jnp.dot(tri, lf, precision=HIGHEST, preferred_element_type=F32)
    if seq_rows > n:
        @pl.when(pl.program_id(0) % (seq_rows // n) == 0)
        def _():
            carry_ref[...] = jnp.zeros_like(carry_ref)

        csum = csum + carry_ref[0:1, :]
        carry_ref[...] = jnp.broadcast_to(csum[n - 1:n, :], carry_ref.shape)
    return lf, csum


def _in_proj_kernel(x_ref, w_ref, ln_g_ref, ln_b_ref, sw_ref, sb_ref, fb_ref, *refs, kv_transposed, n_stacked, seq_rows,
                    sgu_span):
    xb = x_ref[...].astype(BF16)

    def seg(i):
        return _dot(xb, w_ref[:, _IN_OFFS[i]:_IN_OFFS[i] + _IN_SEGS[i]])

    carry_ref = refs[-1]
    if kv_transposed:
        wkvt_ref = refs[0]
        ya_ref, zq_ref, zz_ref, q_ref, kt_ref, vt_ref, ktb_ref, vtb_ref, zg_ref, zs_ref, lf_ref, crow_ref = refs[1 + n_stacked:-1]
        kt = _dot_nt(wkvt_ref[0:FOX_DIM, :], xb)
        vt = _dot_nt(wkvt_ref[FOX_DIM:2 * FOX_DIM, :], xb)
        ktb_ref[...] = kt.astype(BF16)
        vtb_ref[...] = vt.astype(BF16)
        if n_stacked:
            kt_ref[...] = kt
            vt_ref[...] = vt
        else:
            for ref, val in ((kt_ref, kt), (vt_ref, vt)):
                ref[0] = val
                for other in range(1, ref.shape[0]):
                    ref[other] = jnp.zeros_like(val)
    else:
        ya_ref, vn_ref, zq_ref, zz_ref, q_ref, k_ref, v_ref, zg_ref, zs_ref, lf_ref, crow_ref = refs[:-1]
        k_ref[...] = seg(4)
        v_ref[...] = seg(5)
    ya, vn = _spatial_gate(seg(0), ln_g_ref[...], ln_b_ref[...], sw_ref, sb_ref, sgu_span)
    ya_ref[...] = ya.astype(ya_ref.dtype)
    if not kv_transposed:
        vn_ref[...] = vn
    zq_ref[...] = seg(1)
    zz_ref[...] = seg(2)
    q_ref[...] = (seg(3) * (FOX_HD ** -0.5)).astype(q_ref.dtype)
    zg_ref[...] = seg(6)
    zs = seg(7)
    zs_ref[...] = zs
    lf, csum = _forget_sums(zs, fb_ref[...], carry_ref, seq_rows)
    lf_ref[...] = lf.T[0:FOX_HEADS, :] if kv_transposed else lf
    crow_ref[0] = csum.T[0:FOX_HEADS, :]


def _in_proj(x, w, wkvt, sgu_ln_g, sgu_ln_b, sgu_w, sgu_b, f_bias, n_seq, seq_rows, sgu_span, layer=0, kv_stack=None):
    t = x.shape[0]
    tm = PROJ_ROWS
    nb = t // tm
    row = lambda n: pl.BlockSpec((tm, n), lambda i: (i, 0))
    sds = jax.ShapeDtypeStruct
    rows = lambda n, dt=F32: (row(n), sds((t, n), dt))
    consts = (w, sgu_ln_g, sgu_ln_b, sgu_w, sgu_b, f_bias)
    in_specs = [row(D_MODEL)] + [_const_spec(c.shape) for c in consts]
    operands = [x, *consts]
    crow = (pl.BlockSpec((1, FOX_HEADS, tm), lambda i: (i, 0, 0)), sds((nb, FOX_HEADS, tm), F32))
    tail = [rows(N_BRANCH * D_MODEL), rows(SEG_SMALL)]
    aliases = {}
    if wkvt is None:
        outs = [rows(D_A), rows(D_A), rows(SEG_BQKV), rows(SEG_BZ), rows(FOX_DIM), rows(FOX_DIM), rows(FOX_DIM)] + tail
        outs += [rows(LANES), crow]
    else:
        nt = seq_rows // tm
        if kv_stack is None:
            assert layer == 0
            kv_block = pl.BlockSpec((DEPTH, None, FOX_DIM, tm), lambda i: (0, i // nt, 0, i % nt))
        else:
            kv_block = pl.BlockSpec((None, None, FOX_DIM, tm), lambda i: (layer, i // nt, 0, i % nt))
        f32_kv = (kv_block, sds((DEPTH, n_seq, FOX_DIM, seq_rows), F32))
        bf_kv = (pl.BlockSpec((None, None, FOX_DIM, tm), lambda i: (i // nt, i % nt, 0, 0)),
                 sds((n_seq, nt, FOX_DIM, tm), BF16))
        lf = (pl.BlockSpec((None, FOX_HEADS, tm), lambda i: (i // nt, 0, i % nt)), sds((n_seq, FOX_HEADS, seq_rows), F32))
        outs = [rows(D_A, BF16), rows(SEG_BQKV), rows(SEG_BZ), rows(FOX_DIM, BF16), f32_kv, f32_kv, bf_kv, bf_kv] + tail
        outs += [lf, crow]
        in_specs.append(_const_spec(wkvt.shape))
        operands.append(wkvt)
        if kv_stack is not None:
            aliases = {len(operands): 4, len(operands) + 1: 5}
            in_specs += [pl.BlockSpec(memory_space=pl.ANY)] * 2
            operands += list(kv_stack)
    return pl.pallas_call(
        functools.partial(_in_proj_kernel, kv_transposed=wkvt is not None, n_stacked=len(aliases), seq_rows=seq_rows,
                          sgu_span=sgu_span),
        grid=(nb,),
        in_specs=in_specs,
        out_specs=[o[0] for o in outs],
        out_shape=[o[1] for o in outs],
        scratch_shapes=[pltpu.VMEM((SUBLANES, LANES), F32)],
        input_output_aliases=aliases,
        compiler_params=_cparams("arbitrary"),
        name="in_proj",
    )(*operands)


GDN_PAR_ALOG = 0
GDN_PAR_DTB = 1
GDN_PAR_NORM = 2


def _unit_lower_inverse_minus_identity(a):
    n = GDN_CHUNK
    r = _iota((1, n, n), 1)
    c = _iota((1, n, n), 2)
    same16 = (r // 16) == (c // 16)
    same32 = (r // 32) == (c // 32)
    ad = jnp.where(same16, a, 0.0)
    adb = ad.astype(BF16)
    t = -ad
    p = _bmm(adb, adb)
    for step in range(3):
        pb = p.astype(BF16)
        t = t + p + _bmm(-adb if step == 0 else t, pb)
        if step < 2:
            p = _bmm(pb, pb)
    for off in (jnp.where(same32 & ~same16, a, 0.0), jnp.where(~same32, a, 0.0)):
        tb = t.astype(BF16)
        m = off + _bmm(tb, off)
        t = t - m - _bmm(m, tb)
    return t


def _gdn_kernel(zq_ref, zs_ref, zz_ref, conv0_ref, s0_ref, cw_ref, par_ref, y_ref, sfin_ref, xp_scr, s_scr,
                *, ns, nc, rows_in, blocks_per_seq):
    n = GDN_CHUNK
    rows = nc * n
    nch = ns * nc
    blk = pl.program_id(1)

    @pl.when(blk == 0)
    def _():
        xp_scr[:, 0:SUBLANES, :] = conv0_ref[...]
        s_scr[...] = s0_ref[...].reshape(ns * GDN_HEADS, GDN_DK, GDN_DV)

    @pl.when(blk > 0)
    def _():
        xp_scr[:, 0:SUBLANES, :] = xp_scr[:, rows:rows + SUBLANES, :]

    xp_scr[:, SUBLANES:SUBLANES + rows_in, :] = zq_ref[...].reshape(ns, rows_in, CONV_DIM)
    if rows_in < rows:
        xp_scr[:, SUBLANES + rows_in:SUBLANES + rows, :] = jnp.zeros((ns, rows - rows_in, CONV_DIM), F32)

    base = SUBLANES - (CONV_W - 1)
    yc = xp_scr[:, base:base + rows, :] * cw_ref[0:1, :]
    for i in range(1, CONV_W):
        yc = yc + xp_scr[:, base + i:base + i + rows, :] * cw_ref[i:i + 1, :]
    yc = (yc * jax.nn.sigmoid(yc)).reshape(ns * rows, CONV_DIM)

    zs = zs_ref[...].reshape(ns, rows_in, LANES)
    if rows_in < rows:
        zs = jnp.concatenate([zs, jnp.zeros((ns, rows - rows_in, LANES), F32)], axis=1)
    valid = _iota((ns, rows, LANES), 1) < rows_in
    sig = jnp.where(valid, jax.nn.sigmoid(zs), 0.0).reshape(ns * rows, LANES)
    gl = -jnp.exp(par_ref[GDN_PAR_ALOG:GDN_PAR_ALOG + 1, :]) * _softplus(zs + par_ref[GDN_PAR_DTB:GDN_PAR_DTB + 1, :])
    gl = jnp.where(valid, gl, 0.0).reshape(ns * rows, LANES)

    rr = ns * rows
    r = _iota((rr, rr), 0)
    c = _iota((rr, rr), 1)
    tri = jnp.where((r >= c) & ((r // n) == (c // n)), 1.0, 0.0).astype(F32)
    gc = jnp.dot(tri, gl, precision=HIGHEST, preferred_element_type=F32)
    gct = gc.T

    order = [(cc, s, h) for cc in range(nc) for s in range(ns) for h in range(GDN_HEADS)]
    nb = len(order)

    def rows_of(cc, s):
        return slice(s * rows + cc * n, s * rows + (cc + 1) * n)

    def cols3(x, off, width):
        return jnp.stack([x[rows_of(cc, s), off + h * width:off + (h + 1) * width] for cc, s, h in order], axis=0)

    def l2_heads(off, scale):
        heads = []
        for h in range(GDN_HEADS):
            xh = yc[:, off + h * GDN_DK:off + (h + 1) * GDN_DK]
            heads.append(xh * (lax.rsqrt(jnp.sum(xh * xh, axis=-1, keepdims=True) + NORM_EPS) * scale))
        return jnp.concatenate(heads, axis=1)

    q = cols3(l2_heads(0, GDN_DK ** -0.5), 0, GDN_DK)
    k = cols3(l2_heads(GDN_HEADS * GDN_DK, 1.0), 0, GDN_DK)
    v = cols3(yc, 2 * GDN_HEADS * GDN_DK, GDN_DV)
    beta = cols3(sig, SMALL_BETA, 1)
    gcol = cols3(gc, SMALL_A, 1)
    grow = jnp.stack([gct[SMALL_A + h:SMALL_A + h + 1, rows_of(cc, s)] for cc, s, h in order], axis=0)
    glast = gcol[:, n - 1:n, :]

    ri = _iota((1, n, n), 1)
    ci = _iota((1, n, n), 2)
    decay = jnp.exp(jnp.where(ri >= ci, gcol - grow, -jnp.inf))
    egc = jnp.exp(gcol)
    qk_kk = _bmm_nt(jnp.concatenate([q, k], axis=1), k)
    qk = qk_kk[:, :n] * decay
    a = jnp.where(ri > ci, beta * qk_kk[:, n:] * decay, 0.0)
    tinv = _unit_lower_inverse_minus_identity(a)
    rhs = jnp.concatenate([v * beta, k * (beta * egc)], axis=-1)
    sol = rhs + _bmm(tinv, rhs)
    w_val = sol[:, :, :GDN_DV]
    k_cum = sol[:, :, GDN_DV:]
    q_dec = q * egc
    k_tail_t = jnp.swapaxes(k * jnp.exp(glast - gcol), 1, 2)
    g_last = jnp.exp(glast)

    norm_g = par_ref[GDN_PAR_NORM:GDN_PAR_NORM + 1, :]
    per = ns * GDN_HEADS
    for cc in range(nc):
        sl = slice(cc * per, (cc + 1) * per)
        s_old = s_scr[...]
        v_new = w_val[sl] - _bmm(k_cum[sl], s_old)
        o = _bmm(q_dec[sl], s_old) + _bmm(qk[sl], v_new)
        s_scr[...] = s_old * g_last[sl] + _bmm(k_tail_t[sl], v_new)
        nv = min(rows_in - cc * n, n)
        for s in range(ns):
            for h in range(GDN_HEADS):
                r0 = s * rows_in + cc * n
                cols = slice(h * GDN_DV, (h + 1) * GDN_DV)
                zz = zz_ref[r0:r0 + nv, cols]
                ov = o[s * GDN_HEADS + h, :nv]
                gate = zz * jax.nn.sigmoid(zz)
                y_ref[r0:r0 + nv, cols] = (ov * lax.rsqrt(jnp.mean(ov * ov, axis=-1, keepdims=True) + NORM_EPS)
                                           * norm_g * gate).astype(y_ref.dtype)

    @pl.when(blk == blocks_per_seq - 1)
    def _():
        sfin_ref[...] = s_scr[...].reshape(ns, GDN_HEADS, GDN_DK, GDN_DV)


def _gdn(zq, zs, zz, conv0, s0, conv_w, par, n_seq, seq_rows, ns, nc, rows_in, y_dtype):
    t = zq.shape[0]
    bps = seq_rows // rows_in
    rows = nc * GDN_CHUNK

    def row(ncol):
        return pl.BlockSpec((ns * rows_in, ncol), lambda b, j: (b * bps + j, 0))

    state = pl.BlockSpec((ns, GDN_HEADS, GDN_DK, GDN_DV), lambda b, j: (b, 0, 0, 0))
    return pl.pallas_call(
        functools.partial(_gdn_kernel, ns=ns, nc=nc, rows_in=rows_in, blocks_per_seq=bps),
        grid=(n_seq // ns, bps),
        in_specs=[row(CONV_DIM), row(LANES), row(SEG_BZ),
                  pl.BlockSpec((ns, SUBLANES, CONV_DIM), lambda b, j: (b, 0, 0)), state,
                  _const_spec(conv_w.shape), _const_spec(par.shape)],
        out_specs=[row(SEG_BZ), state],
        out_shape=[jax.ShapeDtypeStruct((t, SEG_BZ), y_dtype),
                   jax.ShapeDtypeStruct((n_seq, GDN_HEADS, GDN_DK, GDN_DV), F32)],
        scratch_shapes=[pltpu.VMEM((ns, rows + 2 * SUBLANES, CONV_DIM), F32),
                        pltpu.VMEM((ns * GDN_HEADS, GDN_DK, GDN_DV), F32)],
        compiler_params=_cparams("parallel", "arbitrary"),
        name="gdn",
    )(zq, zs, zz, conv0, s0, conv_w, par)


FOX_TQ = 2048
FOX_SUB = 128
FOX_TK = PROJ_ROWS


def _fox_prompt_kernel(qb_ref, kt_ref, vt_ref, crow_ref, o_ref, *, nq):
    hp = pl.program_id(1)
    nsub = FOX_TQ // FOX_SUB
    below = _iota((FOX_SUB, FOX_TQ), 0) - _iota((FOX_SUB, FOX_TQ), 1)
    ones = jnp.ones((FOX_HD, FOX_TK), BF16)

    def q_block(qi, _):
        r0 = pl.multiple_of(qi * FOX_TQ, FOX_TQ)
        qs = [[qb_ref[pl.ds(r0 + sb * FOX_SUB, FOX_SUB), e * FOX_HD:(e + 1) * FOX_HD] for sb in range(nsub)]
              for e in range(2)]

        per_q = FOX_TQ // FOX_TK

        def step(j0, carry, diagonal):
            out = list(carry)
            cat = lambda xs: xs[0] if len(xs) == 1 else jnp.concatenate(xs, axis=1)
            heads = lambda ref, u, e: ref[j0 + u, e * FOX_HD:(e + 1) * FOX_HD, :]
            kt = [[heads(kt_ref, u, e) for u in range(per_q)] for e in range(2)]
            vt1 = [[jnp.concatenate([heads(vt_ref, u, e), ones], axis=0) for u in range(per_q)] for e in range(2)]
            nrow = [[-crow_ref[j0 + u, pl.ds(2 * hp + e, 1), :] for u in range(per_q)] for e in range(2)]
            seen = lambda sb: per_q if not diagonal else ((sb + 1) * FOX_SUB - 1) // FOX_TK + 1
            chains = [(e, sb) for e in range(2) for sb in range(nsub)]
            s = {}
            for e, sb in chains:
                n = seen(sb)
                s[e, sb] = _dot(qs[e][sb], cat(kt[e][:n])) + cat(nrow[e][:n])
                if diagonal:
                    s[e, sb] = jnp.where(below[:, :n * FOX_TK] + sb * FOX_SUB >= 0, s[e, sb], -jnp.inf)
            m_new = {c: jnp.maximum(carry[c[0] * nsub + c[1]][0], jnp.max(s[c], axis=-1, keepdims=True)) for c in chains}
            p = {c: jnp.exp(s[c] - m_new[c]).astype(BF16) for c in chains}
            for e, sb in chains:
                m, acc = carry[e * nsub + sb]
                acc = jnp.exp(m - m_new[e, sb]) * acc + _dot_nt(p[e, sb], cat(vt1[e][:seen(sb)]))
                out[e * nsub + sb] = (m_new[e, sb], acc)
            return tuple(out)

        init = tuple((jnp.full((FOX_SUB, 1), -jnp.inf, F32), jnp.zeros((FOX_SUB, 2 * FOX_HD), F32))
                     for _ in range(2 * nsub))
        carry = lax.fori_loop(0, qi, lambda i, cr: step(i * per_q, cr, False), init)
        carry = step(qi * per_q, carry, True)
        for sb in range(nsub):
            acc = [carry[e * nsub + sb][1] for e in range(2)]
            o_ref[pl.ds(r0 + sb * FOX_SUB, FOX_SUB), :] = jnp.concatenate(
                [a[:, :FOX_HD] / a[:, FOX_HD:] for a in acc], axis=-1).astype(o_ref.dtype)
        return 0

    lax.fori_loop(0, nq, q_block, 0)


def _fox_prompt(qb, ktb, vtb, crow, n_seq, seq_rows):
    t = qb.shape[0]
    nkv = seq_rows // FOX_TK
    qspec = pl.BlockSpec((seq_rows, LANES), lambda b, h: (b, h))
    kvspec = pl.BlockSpec((None, nkv, LANES, FOX_TK), lambda b, h: (b, 0, h, 0))
    return pl.pallas_call(
        functools.partial(_fox_prompt_kernel, nq=seq_rows // FOX_TQ),
        grid=(n_seq, FOX_HEADS // 2),
        in_specs=[qspec, kvspec, kvspec,
                  pl.BlockSpec((None, nkv, FOX_HEADS, FOX_TK), lambda b, h: (b, 0, 0, 0))],
        out_specs=qspec,
        out_shape=jax.ShapeDtypeStruct((t, FOX_DIM), BF16),
        compiler_params=_cparams("parallel", "parallel"),
        name="fox_prompt",
    )(qb, ktb, vtb, crow)


FS_GROUP = 8
FS_SLOTS = 4
FS_ROWS = FOX_HEADS * SUBLANES


class _PagedAttention:
    def __init__(self, pt_ref, lfc_ref, ck_ref, cv_ref, kbuf, vbuf, lbuf, sem, n_seq, n_pages, layer):
        self.pt_ref, self.lfc_ref, self.ck_ref, self.cv_ref = pt_ref, lfc_ref, ck_ref, cv_ref
        self.kbuf, self.vbuf, self.lbuf, self.sem = kbuf, vbuf, lbuf, sem
        self.n_pages, self.layer = n_pages, layer
        self.n_groups = n_pages // FS_GROUP
        self.total = n_seq * self.n_groups
        rr = _iota((FS_ROWS, FOX_DIM), 0)
        cc = _iota((FS_ROWS, FOX_DIM), 1)
        self.own = (rr // SUBLANES) == (cc // FOX_HD)
        pr = _iota((PAGE_SIZE, PAGE_SIZE), 0)
        pc = _iota((PAGE_SIZE, PAGE_SIZE), 1)
        self.later = jnp.where(pr > pc, 1.0, 0.0).astype(F32)

    def copies(self, f, slot):
        b = f // self.n_groups
        gi = f % self.n_groups
        out = []
        for u in range(FS_GROUP):
            page = self.pt_ref[b, self.n_pages - 1 - (gi * FS_GROUP + u)]
            lanes = slice(u * PAGE_SIZE, (u + 1) * PAGE_SIZE)
            heads = slice(u * FOX_HEADS, (u + 1) * FOX_HEADS)
            out.append(pltpu.make_async_copy(self.ck_ref.at[self.layer, page], self.kbuf.at[slot, :, lanes],
                                             self.sem.at[0, slot]))
            out.append(pltpu.make_async_copy(self.cv_ref.at[self.layer, page], self.vbuf.at[slot, :, lanes],
                                             self.sem.at[1, slot]))
            out.append(pltpu.make_async_copy(self.lfc_ref.at[self.layer, page], self.lbuf.at[slot, heads, :],
                                             self.sem.at[2, slot]))
        return out

    def prime(self):
        for f0 in range(FS_SLOTS - 1):
            for cp in self.copies(f0, f0):
                cp.start()

    @staticmethod
    def rows8(x):
        return jnp.concatenate([jnp.broadcast_to(x[h:h + 1, :], (SUBLANES, x.shape[1])) for h in range(FOX_HEADS)], axis=0)

    @staticmethod
    def update(carry, s, vbf, v_transposed):
        m, l, acc = carry
        m_new = jnp.maximum(m, jnp.max(s, axis=-1, keepdims=True))
        alpha = jnp.exp(m - m_new)
        p = jnp.exp(s - m_new)
        l = alpha * l + jnp.sum(p, axis=-1, keepdims=True)
        pv = _dot_nt(p.astype(BF16), vbf) if v_transposed else _dot(p.astype(BF16), vbf)
        return m_new, l, alpha * acc + pv

    def begin(self, q):
        qrep = jnp.concatenate([q.astype(BF16)] * FOX_HEADS, axis=0)
        qbd = jnp.where(self.own, qrep, jnp.zeros_like(qrep))
        state = (jnp.full((FS_ROWS, 1), -jnp.inf, F32), jnp.zeros((FS_ROWS, 1), F32),
                 jnp.zeros((FS_ROWS, FOX_DIM), F32), jnp.zeros((FOX_HEADS, 1), F32))
        return qbd, state

    def group(self, f, qbd, state):
        m, l, acc, tail = state
        slot = f % FS_SLOTS
        nxt = f + FS_SLOTS - 1

        @pl.when(nxt < self.total)
        def _():
            for cp in self.copies(nxt, nxt % FS_SLOTS):
                cp.start()

        for cp in self.copies(f, slot):
            cp.wait()
        lf = self.lbuf[slot]
        suffix = jnp.dot(lf, self.later, precision=HIGHEST, preferred_element_type=F32)
        tot = jnp.sum(lf, axis=-1, keepdims=True)
        bias = []
        for u in range(FS_GROUP):
            rows = slice(u * FOX_HEADS, (u + 1) * FOX_HEADS)
            bias.append(self.rows8(suffix[rows] + tail))
            tail = tail + tot[rows]
        s = _dot(qbd, self.kbuf[slot].astype(BF16)) + jnp.concatenate(bias, axis=1)
        m, l, acc = self.update((m, l, acc), s, self.vbuf[slot].astype(BF16), True)
        return m, l, acc, tail

    def finish(self, qbd, state, kn, vn, crow):
        m, l, acc, _ = state
        pad = jnp.zeros((PAGE_SIZE - SUBLANES, FOX_DIM), F32)
        knp = jnp.concatenate([kn, pad], axis=0).astype(BF16)
        vnp = jnp.concatenate([vn, pad], axis=0).astype(BF16)
        tok = _iota((FS_ROWS, PAGE_SIZE), 0) % SUBLANES
        pos = _iota((FS_ROWS, PAGE_SIZE), 1)
        s = jnp.where(pos <= tok, _dot_nt(qbd, knp) - self.rows8(crow), -jnp.inf)
        m, l, acc = self.update((m, l, acc), s, vnp, False)
        o = acc / l
        out = jnp.zeros((SUBLANES, FOX_DIM), F32)
        hl = _iota((SUBLANES, FOX_DIM), 1) // FOX_HD
        for h in range(FOX_HEADS):
            out = jnp.where(hl == h, o[h * SUBLANES:(h + 1) * SUBLANES, :], out)
        return out


def _fox_sample_kernel(pt_ref, q_ref, kn_ref, vn_ref, crow_ref, lfc_ref, ck_ref, cv_ref, o_ref,
                       kbuf, vbuf, lbuf, sem, *, n_seq, n_pages, layer):
    att = _PagedAttention(pt_ref, lfc_ref, ck_ref, cv_ref, kbuf, vbuf, lbuf, sem, n_seq, n_pages, layer)
    att.prime()

    def sequence(b, _):
        qbd, state = att.begin(q_ref[b])
        state = lax.fori_loop(0, att.n_groups, lambda gi, st: att.group(b * att.n_groups + gi, qbd, st), state)
        o_ref[b] = att.finish(qbd, state, kn_ref[b], vn_ref[b], crow_ref[b])
        return 0

    lax.fori_loop(0, n_seq, sequence, 0)


def _fox_sample(page_table, q, kn, vn, crow, lfc, ck, cv, layer):
    n_seq, n_pages = page_table.shape
    gw = FS_GROUP * PAGE_SIZE
    whole = lambda a: pl.BlockSpec(a.shape, lambda i, pt: (0,) * a.ndim)
    any_spec = pl.BlockSpec(memory_space=pl.ANY)
    grid_spec = pltpu.PrefetchScalarGridSpec(
        num_scalar_prefetch=1,
        grid=(1,),
        in_specs=[whole(q), whole(kn), whole(vn), whole(crow), any_spec, any_spec, any_spec],
        out_specs=pl.BlockSpec((n_seq, SUBLANES, FOX_DIM), lambda i, pt: (0, 0, 0)),
        scratch_shapes=[pltpu.VMEM((FS_SLOTS, FOX_DIM, gw), F32),
                        pltpu.VMEM((FS_SLOTS, FOX_DIM, gw), F32),
                        pltpu.VMEM((FS_SLOTS, FS_GROUP * FOX_HEADS, PAGE_SIZE), F32),
                        pltpu.SemaphoreType.DMA((3, FS_SLOTS))],
    )
    return pl.pallas_call(
        functools.partial(_fox_sample_kernel, n_seq=n_seq, n_pages=n_pages, layer=layer),
        grid_spec=grid_spec,
        out_shape=jax.ShapeDtypeStruct((n_seq, SUBLANES, FOX_DIM), F32),
        compiler_params=_cparams("arbitrary"),
        name="fox_sample",
    )(page_table, q, kn, vn, crow, lfc, ck, cv)


def _merge_kernel(ya_ref, yb_ref, yc_ref, zg_ref, x_ref, gb_ref, pa_ref, pb_ref, pc_ref, wo_ref, g_ref, b_ref, o_ref):
    merged = None
    for i, (y_ref, p_ref) in enumerate(((ya_ref, pa_ref), (yb_ref, pb_ref), (yc_ref, pc_ref))):
        cols = slice(i * D_MODEL, (i + 1) * D_MODEL)
        gate = jax.nn.sigmoid(zg_ref[:, cols] + gb_ref[:, cols])
        term = gate * _dot(y_ref[...].astype(BF16), p_ref[...])
        merged = term if merged is None else merged + term
    mix = _dot(merged.astype(BF16), wo_ref[...])
    o_ref[...] = _layer_norm(ALPHA * x_ref[...] + mix, g_ref[...], b_ref[...])


def _merge(ya, yb, yc, zg, x, gate_bias, pa, pb, pc, wo, g, b, layer, tm):
    t = x.shape[0]
    row = lambda n: pl.BlockSpec((tm, n), lambda i: (i, 0))
    consts = (gate_bias, pa, pb, pc, wo, g, b)
    return pl.pallas_call(
        _merge_kernel,
        grid=(t // tm,),
        in_specs=[row(D_A), row(SEG_BZ), row(FOX_DIM), row(N_BRANCH * D_MODEL), row(D_MODEL)]
        + [_layer_spec(c.shape, layer) for c in consts],
        out_specs=row(D_MODEL),
        out_shape=jax.ShapeDtypeStruct((t, D_MODEL), F32),
        compiler_params=_cparams("parallel"),
        name="merge",
    )(ya, yb, yc, zg, x, *consts)


_W_IN_K = 2 * D_A + CONV_DIM + 2 * GDN_HEADS + SEG_BZ + FOX_DIM


def _regroup_w_in(w_in):
    o_bqkv = 2 * D_A
    o_beta = o_bqkv + CONV_DIM
    o_a = o_beta + GDN_HEADS
    o_z = o_a + GDN_HEADS
    o_cqkv = o_z + SEG_BZ
    o_f = o_cqkv + 3 * FOX_DIM
    o_gate = o_f + FOX_HEADS
    small = jnp.concatenate([w_in[:, o_f:o_gate], w_in[:, o_beta:o_a], w_in[:, o_a:o_z],
                             jnp.zeros((D_MODEL, SEG_SMALL - FOX_HEADS - 2 * GDN_HEADS), w_in.dtype)], axis=1)
    return jnp.concatenate([w_in[:, :o_beta], w_in[:, o_z:o_cqkv], w_in[:, o_cqkv:o_f], w_in[:, o_gate:], small],
                           axis=1).astype(BF16)


def _lane_row(vals, offset):
    return jnp.zeros((1, LANES), F32).at[0, offset:offset + vals.shape[0]].set(vals.astype(F32))


def _layer_params(l, p):
    par = jnp.zeros((SUBLANES, LANES), F32)
    par = par.at[GDN_PAR_ALOG, SMALL_A:SMALL_A + GDN_HEADS].set(p["gdn_a_log"][l])
    par = par.at[GDN_PAR_DTB, SMALL_A:SMALL_A + GDN_HEADS].set(p["gdn_dt_bias"][l])
    par = par.at[GDN_PAR_NORM, :].set(p["gdn_norm_g"][l])
    conv_w = jnp.concatenate([p["gdn_conv_w"][l], jnp.zeros((SUBLANES - CONV_W, CONV_DIM), F32)], axis=0)
    row = lambda a: a[l].reshape(1, -1)
    return dict(
        w_in=_regroup_w_in(p["w_in"][l]),
        wkvt=p["w_in"][l][:, _W_IN_K:_W_IN_K + 2 * FOX_DIM].T.astype(BF16),
        sgu_ln=(row(p["sgu_ln_g"]), row(p["sgu_ln_b"])),
        sgu_w=p["sgu_w"][l], sgu_b=p["sgu_b"][l],
        f_bias=_lane_row(p["fox_f_bias"][l], SMALL_F),
        conv_w=conv_w, gdn_par=par,
    )


def _mixer_front(x, lp, n_seq, seq_rows, conv0, s0, long_seq, layer=0, kv_stack=None):
    if seq_rows >= SGU_CHUNK:
        sgu_w, sgu_b, span = lp["sgu_w"], lp["sgu_b"], SGU_CHUNK
    else:
        rep = SGU_CHUNK // seq_rows
        sgu_w = jnp.tile(lp["sgu_w"][:, :seq_rows, :seq_rows], (1, rep, rep))
        sgu_b = jnp.tile(lp["sgu_b"][:, :seq_rows], (1, rep))
        span = seq_rows
    proj = functools.partial(_in_proj, x, lp["w_in"], sgu_ln_g=lp["sgu_ln"][0], sgu_ln_b=lp["sgu_ln"][1], sgu_w=sgu_w,
                             sgu_b=sgu_b[:, :, None], f_bias=lp["f_bias"], n_seq=n_seq, seq_rows=seq_rows, sgu_span=span)
    if long_seq:
        ya, zq, zz, q, k, v, kb, vb, zg, zs, lf, crow = proj(wkvt=lp["wkvt"], layer=layer, kv_stack=kv_stack)
        vn = None
    else:
        ya, vn, zq, zz, q, k, v, zg, zs, lf, crow = proj(wkvt=None)
        kb = vb = None
    if long_seq:
        ns, nc, rows_in = 1, 4, 4 * GDN_CHUNK
    else:
        ns, nc, rows_in = 8, 1, seq_rows
    yb, s_fin = _gdn(zq, zs, zz, conv0, s0, lp["conv_w"], lp["gdn_par"], n_seq, seq_rows, ns, nc, rows_in,
                     BF16 if long_seq else F32)
    return dict(ya=ya, vn=vn, yb=yb, s_fin=s_fin, zq=zq, zg=zg, q=q, k=k, v=v, kb=kb, vb=vb, lf=lf, crow=crow)


def kernel(x_prompt, x_sample, cache_k, cache_v, cache_logf, page_table, state_conv, state_gdn,
           ln1_g, ln1_b, ffn1_w1, ffn1_w3, ffn1_w2, w_in, sgu_ln_g, sgu_ln_b, sgu_w, sgu_b,
           gdn_conv_w, gdn_a_log, gdn_dt_bias, gdn_norm_g, fox_f_bias, gate_bias,
           proj_a, proj_b, proj_c, w_out, ln2_g, ln2_b, ffn2_w1, ffn2_w3, ffn2_w2, ln3_g, ln3_b):
    p = dict(w_in=w_in, sgu_ln_g=sgu_ln_g, sgu_ln_b=sgu_ln_b, sgu_w=sgu_w, sgu_b=sgu_b, gdn_conv_w=gdn_conv_w,
             gdn_a_log=gdn_a_log, gdn_dt_bias=gdn_dt_bias, gdn_norm_g=gdn_norm_g, fox_f_bias=fox_f_bias)
    bp, sp, _ = x_prompt.shape
    db, ds, _ = x_sample.shape
    tp, ts = bp * sp, db * ds
    n_pool = cache_k.shape[1]
    tm_p = 1024 if tp % 1024 == 0 else 256
    tm_s = ts
    assert ts == PROJ_ROWS and sp % FOX_TQ == 0 and ds == SUBLANES and cache_k.shape[2] == PAGE_SIZE
    assert page_table.shape[1] % FS_GROUP == 0 and db % 8 == 0

    bf = lambda a: a.astype(BF16)
    rows3 = lambda a: a[:, None, :]
    ffn1 = (bf(ffn1_w1), bf(ffn1_w3), bf(ffn1_w2), rows3(ln1_g), rows3(ln1_b))
    ffn2 = (bf(ffn2_w1), bf(ffn2_w3), bf(ffn2_w2), rows3(ln3_g), rows3(ln3_b))
    mrg = (rows3(gate_bias), bf(proj_a), bf(proj_b), bf(proj_c), bf(w_out), rows3(ln2_g), rows3(ln2_b))

    yp = x_prompt.reshape(tp, D_MODEL)
    ys = x_sample.reshape(ts, D_MODEL)
    outs = [[] for _ in range(11)]
    kv_stack = None
    conv_pad = jnp.zeros((SUBLANES - (CONV_W - 1), CONV_DIM), F32)
    ck = jnp.transpose(cache_k, (0, 1, 3, 4, 2)).reshape(DEPTH, n_pool, FOX_DIM, PAGE_SIZE)
    cv = jnp.transpose(cache_v, (0, 1, 3, 4, 2)).reshape(DEPTH, n_pool, FOX_DIM, PAGE_SIZE)
    lfc = jnp.swapaxes(cache_logf, 2, 3)
    for l in range(DEPTH):
        lp = _layer_params(l, p)

        yp = _ffn_ln(yp, *ffn1, l, tm_p)
        conv0 = jnp.zeros((bp, SUBLANES, CONV_DIM), F32)
        s0 = jnp.zeros((bp, GDN_HEADS, GDN_DK, GDN_DV), F32)
        f = _mixer_front(yp, lp, bp, sp, conv0, s0, True, l, kv_stack)
        kv_stack = (f["k"], f["v"])
        nkv = sp // FOX_TK
        crow = f["crow"].reshape(bp, nkv, FOX_HEADS, FOX_TK)
        yc = _fox_prompt(f["q"], f["kb"], f["vb"], crow, bp, sp)
        yp = _merge(f["ya"], f["yb"], yc, f["zg"], yp, *mrg, l, min(tm_p, 512))
        yp = _ffn_ln(yp, *ffn2, l, tm_p)
        outs[2].append(jnp.transpose(f["lf"], (0, 2, 1)))
        outs[3].append(f["zq"].reshape(bp, sp, CONV_DIM)[:, sp - (CONV_W - 1):])
        outs[4].append(f["s_fin"])

        ys = _ffn_ln(ys, *ffn1, l, tm_s)
        conv0 = jnp.concatenate([jnp.broadcast_to(conv_pad, (db,) + conv_pad.shape), state_conv[l]], axis=1)
        f = _mixer_front(ys, lp, db, ds, conv0, state_gdn[l], False)
        crow = jnp.pad(f["crow"][0].reshape(FOX_HEADS, db, ds).transpose(1, 0, 2), ((0, 0), (0, 0), (0, LANES - ds)))
        per_seq = lambda a: a.reshape(db, ds, FOX_DIM)
        yc = _fox_sample(page_table, per_seq(f["q"]), per_seq(f["k"]), per_seq(f["v"]), crow, lfc, ck, cv, l)
        ys = _merge(f["ya"], f["yb"], yc.reshape(ts, FOX_DIM), f["zg"], ys, *mrg, l, tm_s)
        ys = _ffn_ln(ys, *ffn2, l, tm_s)
        outs[5].append(f["k"].reshape(db, ds, FOX_HEADS, FOX_HD))
        outs[6].append(f["v"].reshape(db, ds, FOX_HEADS, FOX_HD))
        outs[7].append(f["lf"][:, SMALL_F:SMALL_F + FOX_HEADS].reshape(db, ds, FOX_HEADS))
        outs[8].append(f["zq"].reshape(db, ds, CONV_DIM)[:, ds - (CONV_W - 1):])
        outs[9].append(f["s_fin"])
        outs[10].append(f["vn"].reshape(db, ds, D_A))

    to_bshd = lambda a: jnp.transpose(a.reshape(DEPTH, bp, FOX_HEADS, FOX_HD, sp), (0, 1, 4, 2, 3))
    stacked = [to_bshd(kv_stack[0]), to_bshd(kv_stack[1])] + [jnp.stack(o) for o in outs[2:]]
    return (yp.reshape(bp, sp, D_MODEL), ys.reshape(db, ds, D_MODEL)) + tuple(stacked)
```
